```python
import math
import jax, jax.numpy as jnp
from jax import lax
import numpy as np

D_MODEL = 2048
BATCH = 4
SEQ = 2048
DEPTH = 4

N_MIXERS = 2
HEAD_DIM = 128
FOX_HEADS = D_MODEL // HEAD_DIM
DIFF_HEADS = D_MODEL // (2 * HEAD_DIM)
D_FF = 256 * ((8 * D_MODEL // 3 + 255) // 256)
CONV_WIDTH = 3
NUM_BUCKETS = 32
MAX_DISTANCE = 128
Q_BLOCK = 128
NORM_EPS = 1e-6
SUBLN_EPS = 1e-5
N_FOX = (DEPTH + 1) // 2
N_DIFF = DEPTH // 2

kernel_name = "hybrid_fox_diffattn_convffn"


def rms_norm(x, g, eps=NORM_EPS):
    xf = x.astype(jnp.float32)
    y = xf * lax.rsqrt(jnp.mean(xf * xf, axis=-1, keepdims=True) + eps)
    return (y * g.astype(jnp.float32)).astype(x.dtype)


def t5_bucket(dist):
    n = jnp.maximum(dist, 0)
    max_exact = NUM_BUCKETS // 2
    nf = jnp.maximum(n, 1).astype(jnp.float32)
    large = max_exact + (jnp.log(nf / max_exact) / math.log(MAX_DISTANCE / max_exact)
                         * (NUM_BUCKETS - max_exact)).astype(jnp.int32)
    large = jnp.minimum(large, NUM_BUCKETS - 1)
    return jnp.where(n < max_exact, n, large)


def fox_attention(h, w_in, b_f, qk_norm, w_out):
    B, S, _ = h.shape
    nb = S // Q_BLOCK
    proj = h @ w_in
    q, k, v, gate, f_logit = jnp.split(proj, [D_MODEL, 2 * D_MODEL, 3 * D_MODEL, 4 * D_MODEL], axis=-1)
    q = rms_norm(q.reshape(B, S, FOX_HEADS, HEAD_DIM), qk_norm[0])
    k = rms_norm(k.reshape(B, S, FOX_HEADS, HEAD_DIM), qk_norm[1])
    v = v.reshape(B, S, FOX_HEADS, HEAD_DIM)
    log_f = jax.nn.log_sigmoid((f_logit + b_f).astype(jnp.float32))
    c = jnp.cumsum(log_f, axis=1).transpose(0, 2, 1)
    q_blk = jnp.moveaxis(q.reshape(B, nb, Q_BLOCK, FOX_HEADS, HEAD_DIM), 1, 0)
    c_blk = jnp.moveaxis(c.reshape(B, FOX_HEADS, nb, Q_BLOCK), 2, 0)
    key_pos = jnp.arange(S)
    scale = HEAD_DIM ** -0.5

    def block(args):
        qb, cb, bi = args
        q_pos = bi * Q_BLOCK + jnp.arange(Q_BLOCK)
        logits = jnp.einsum('bqhd,bkhd->bhqk', qb, k).astype(jnp.float32) * scale
        decay = cb[..., :, None] - c[..., None, :]
        causal = key_pos[None, :] <= q_pos[:, None]
        logits = jnp.where(causal, logits + decay, -jnp.inf)
        p = jax.nn.softmax(logits, axis=-1).astype(v.dtype)
        return jnp.einsum('bhqk,bkhd->bqhd', p, v)

    o = lax.map(block, (q_blk, c_blk, jnp.arange(nb)))
    o = jnp.moveaxis(o, 0, 1).reshape(B, S, D_MODEL)
    o = o * jax.nn.sigmoid(gate)
    return o @ w_out


def diff_attention(h, w_in, lam, subln, w_out, rel_bias, lambda_init):
    B, S, _ = h.shape
    nb = S // Q_BLOCK
    q, k, v = jnp.split(h @ w_in, 3, axis=-1)
    q = q.reshape(B, S, DIFF_HEADS, 2, HEAD_DIM)
    k = k.reshape(B, S, DIFF_HEADS, 2, HEAD_DIM)
    v = v.reshape(B, S, DIFF_HEADS, 2 * HEAD_DIM)
    lamf = lam.astype(jnp.float32)
    lam_full = (jnp.exp(jnp.sum(lamf[0] * lamf[1])) - jnp.exp(jnp.sum(lamf[2] * lamf[3]))
                + lambda_init)
    q_blk = jnp.moveaxis(q.reshape(B, nb, Q_BLOCK, DIFF_HEADS, 2, HEAD_DIM), 1, 0)
    key_pos = jnp.arange(S)
    scale = HEAD_DIM ** -0.5

    def block(args):
        qb, bi = args
        q_pos = bi * Q_BLOCK + jnp.arange(Q_BLOCK)
        logits = jnp.einsum('bqhjd,bkhjd->bhjqk', qb, k).astype(jnp.float32) * scale
        bias = rel_bias[t5_bucket(q_pos[:, None] - key_pos[None, :])]
        bias = bias.transpose(2, 3, 0, 1).astype(jnp.float32)
        causal = key_pos[None, :] <= q_pos[:, None]
        logits = jnp.where(causal, logits + bias, -jnp.inf)
        p = jax.nn.softmax(logits, axis=-1)
        a = p[:, :, 0] - lam_full * p[:, :, 1]
        return jnp.einsum('bhqk,bkhd->bqhd', a.astype(v.dtype), v)

    o = lax.map(block, (q_blk, jnp.arange(nb)))
    o = jnp.moveaxis(o, 0, 1).reshape(B, S, DIFF_HEADS, 2 * HEAD_DIM)
    o = rms_norm(o, subln, SUBLN_EPS) * (1.0 - lambda_init)
    return o.reshape(B, S, D_MODEL) @ w_out


def conv_ffn(h, w_in, conv_w, conv_b, w_out):
    g, u = jnp.split(h @ w_in, 2, axis=-1)
    g = lax.conv_general_dilated(g, conv_w[:, None, :], window_strides=(1,),
                                 padding=[(CONV_WIDTH - 1, 0)],
                                 dimension_numbers=('NWC', 'WIO', 'NWC'),
                                 feature_group_count=D_FF) + conv_b
    return (jax.nn.gelu(g, approximate=False) * u) @ w_out


def setup_inputs(seed: int = 0) -> dict:
    key = jax.random.key(seed)
    ks = jax.random.split(key, 20)
    n = jax.random.normal
    D, F = D_MODEL, D_FF
    return {
        "x": n(ks[0], (BATCH, SEQ, D), jnp.float32),
        "attn_norm": 1.0 + 0.02 * n(ks[1], (DEPTH, D), jnp.float32),
        "ffn_norm": 1.0 + 0.02 * n(ks[2], (DEPTH, D), jnp.float32),
        "final_norm": 1.0 + 0.02 * n(ks[3], (D,), jnp.float32),
        "fox_w_in": n(ks[4], (N_FOX, D, 4 * D + FOX_HEADS), jnp.float32) * D ** -0.5,
        "fox_b_f": jax.random.uniform(ks[5], (N_FOX, FOX_HEADS), jnp.float32, 1.0, 4.0),
        "fox_qk_norm": 1.0 + 0.02 * n(ks[6], (N_FOX, 2, HEAD_DIM), jnp.float32),
        "fox_w_out": n(ks[7], (N_FOX, D, D), jnp.float32) * D ** -0.5,
        "diff_w_in": n(ks[8], (N_DIFF, D, 3 * D), jnp.float32) * D ** -0.5,
        "diff_lambda": 0.1 * n(ks[9], (N_DIFF, 4, HEAD_DIM), jnp.float32),
        "diff_subln": 1.0 + 0.02 * n(ks[10], (N_DIFF, 2 * HEAD_DIM), jnp.float32),
        "diff_w_out": n(ks[11], (N_DIFF, D, D), jnp.float32) * D ** -0.5,
        "rel_bias": 0.5 * n(ks[12], (NUM_BUCKETS, DIFF_HEADS, 2), jnp.float32),
        "ffn_w_in": n(ks[13], (DEPTH, D, 2 * F), jnp.float32) * D ** -0.5,
        "ffn_conv_w": n(ks[14], (DEPTH, CONV_WIDTH, F), jnp.float32) * CONV_WIDTH ** -0.5,
        "ffn_conv_b": 0.02 * n(ks[15], (DEPTH, F), jnp.float32),
        "ffn_w_out": n(ks[16], (DEPTH, F, D), jnp.float32) * F ** -0.5,
    }


def reference(x, attn_norm, ffn_norm, final_norm, fox_w_in, fox_b_f, fox_qk_norm, fox_w_out,
              diff_w_in, diff_lambda, diff_subln, diff_w_out, rel_bias,
              ffn_w_in, ffn_conv_w, ffn_conv_b, ffn_w_out):
    h = x
    for i in range(DEPTH):
        j = i // N_MIXERS
        a_in = rms_norm(h, attn_norm[i])
        if i % N_MIXERS == 0:
            h = h + fox_attention(a_in, fox_w_in[j], fox_b_f[j], fox_qk_norm[j], fox_w_out[j])
        else:
            lambda_init = 0.8 - 0.6 * math.exp(-0.3 * i)
            h = h + diff_attention(a_in, diff_w_in[j], diff_lambda[j], diff_subln[j],
                                   diff_w_out[j], rel_bias, lambda_init)
        h = h + conv_ffn(rms_norm(h, ffn_norm[i]), ffn_w_in[i], ffn_conv_w[i], ffn_conv_b[i], ffn_w_out[i])
    return rms_norm(h, final_norm)
```

```python
import functools
import math

import jax
import jax.numpy as jnp
import numpy as np
from jax import lax
from jax.experimental import pallas as pl
from jax.experimental.pallas import tpu as pltpu

D_MODEL = 2048
BATCH = 4
SEQ = 2048
TOKENS = BATCH * SEQ
DEPTH = 4
HEAD_DIM = 128
FOX_HEADS = D_MODEL // HEAD_DIM
DIFF_HEADS = D_MODEL // (2 * HEAD_DIM)
D_FF = 5632
CONV_WIDTH = 3
NUM_BUCKETS = 32
MAX_DISTANCE = 128
NORM_EPS = 1e-6
SUBLN_EPS = 1e-5

LANES = 128
VMEM_LIMIT = 56 * 1024 * 1024
NEG = -1e30
LOG2E = math.log2(math.e)
Q_SCALE = HEAD_DIM ** -0.5 * LOG2E

TQ = 256
TK = 256
NQ = SEQ // TQ

F32 = jnp.float32
BF16 = jnp.bfloat16


def _params(*sem):
    return pltpu.CompilerParams(dimension_semantics=sem, vmem_limit_bytes=VMEM_LIMIT)


def _rmsnorm_kernel(x_ref, g_ref, o_ref):
    x = x_ref[...]
    ms = jnp.mean(x * x, axis=-1, keepdims=True)
    o_ref[...] = (x * lax.rsqrt(ms + NORM_EPS) * g_ref[...]).astype(o_ref.dtype)


def _rmsnorm(h, g, out_dtype):
    tm = 512
    return pl.pallas_call(
        _rmsnorm_kernel,
        grid=(TOKENS // tm,),
        in_specs=[pl.BlockSpec((tm, D_MODEL), lambda m: (m, 0)),
                  pl.BlockSpec((1, D_MODEL), lambda m: (0, 0))],
        out_specs=pl.BlockSpec((tm, D_MODEL), lambda m: (m, 0)),
        out_shape=jax.ShapeDtypeStruct((TOKENS, D_MODEL), out_dtype),
        compiler_params=_params("arbitrary"),
        name="rmsnorm",
    )(h, g.reshape(1, D_MODEL))


def _head_rmsnorm(acc, gain, mult):
    ms = jnp.mean(acc * acc, axis=-1, keepdims=True)
    return acc * lax.rsqrt(ms + NORM_EPS) * (gain * mult)


def _fox_proj_kernel(x_ref, w_ref, qkn_ref, o_ref, wb_ref, *, tn):
    n = pl.program_id(0)

    @pl.when(pl.program_id(1) == 0)
    def _():
        wb_ref[...] = w_ref[...].astype(BF16)

    acc = jnp.dot(x_ref[...], wb_ref[...], preferred_element_type=F32)
    per = D_MODEL // tn

    def normed(row, mult):
        gain = qkn_ref[row:row + 1, :]
        for c in range(tn // HEAD_DIM):
            sl = slice(c * HEAD_DIM, (c + 1) * HEAD_DIM)
            o_ref[:, sl] = _head_rmsnorm(acc[:, sl], gain, mult).astype(BF16)

    @pl.when(n < per)
    def _():
        normed(0, Q_SCALE)

    @pl.when((n >= per) & (n < 2 * per))
    def _():
        normed(1, 1.0)

    @pl.when((n >= 2 * per) & (n < 3 * per))
    def _():
        o_ref[...] = acc.astype(BF16)

    @pl.when(n >= 3 * per)
    def _():
        o_ref[...] = jax.nn.sigmoid(acc).astype(BF16)


def _fox_proj(a, w_in, layer, qk_norm):
    tm, tn = 512, 1024
    n_out = 4 * D_MODEL
    return pl.pallas_call(
        functools.partial(_fox_proj_kernel, tn=tn),
        grid=(n_out // tn, TOKENS // tm),
        in_specs=[pl.BlockSpec((tm, D_MODEL), lambda n, m: (m, 0)),
                  pl.BlockSpec((None, D_MODEL, tn), lambda n, m: (layer, 0, n)),
                  pl.BlockSpec((2, HEAD_DIM), lambda n, m: (0, 0))],
        out_specs=pl.BlockSpec((tm, tn), lambda n, m: (m, n)),
        out_shape=jax.ShapeDtypeStruct((TOKENS, n_out), BF16),
        scratch_shapes=[pltpu.VMEM((D_MODEL, tn), BF16)],
        compiler_params=_params("arbitrary", "arbitrary"),
        name="fox_proj",
    )(a, w_in, qk_norm)


def _diff_proj_kernel(x_ref, w_ref, o_ref, wb_ref, *, tn):
    n = pl.program_id(0)

    @pl.when(pl.program_id(1) == 0)
    def _():
        wb_ref[...] = w_ref[...].astype(BF16)

    acc = jnp.dot(x_ref[...], wb_ref[...], preferred_element_type=F32)
    mult = jnp.where(n < D_MODEL // tn, Q_SCALE, 1.0).astype(F32)
    o_ref[...] = (acc * mult).astype(BF16)


def _diff_proj(a, w_in, layer):
    tm, tn = 512, 1024
    n_out = 3 * D_MODEL
    return pl.pallas_call(
        functools.partial(_diff_proj_kernel, tn=tn),
        grid=(n_out // tn, TOKENS // tm),
        in_specs=[pl.BlockSpec((tm, D_MODEL), lambda n, m: (m, 0)),
                  pl.BlockSpec((None, D_MODEL, tn), lambda n, m: (layer, 0, n))],
        out_specs=pl.BlockSpec((tm, tn), lambda n, m: (m, n)),
        out_shape=jax.ShapeDtypeStruct((TOKENS, n_out), BF16),
        scratch_shapes=[pltpu.VMEM((D_MODEL, tn), BF16)],
        compiler_params=_params("arbitrary", "arbitrary"),
        name="diff_proj",
    )(a, w_in)


def _residual_matmul_kernel(x_ref, w_ref, h_ref, o_ref, wb_ref):
    @pl.when(pl.program_id(1) == 0)
    def _():
        wb_ref[...] = w_ref[...].astype(BF16)

    acc = jnp.dot(x_ref[...], wb_ref[...], preferred_element_type=F32)
    o_ref[...] = h_ref[...] + acc


def _residual_matmul(x, w, layer, h, tm, tn):
    k = x.shape[1]
    return pl.pallas_call(
        _residual_matmul_kernel,
        grid=(D_MODEL // tn, TOKENS // tm),
        in_specs=[pl.BlockSpec((tm, k), lambda n, m: (m, 0)),
                  pl.BlockSpec((None, k, tn), lambda n, m: (layer, 0, n)),
                  pl.BlockSpec((tm, tn), lambda n, m: (m, n))],
        out_specs=pl.BlockSpec((tm, tn), lambda n, m: (m, n)),
        out_shape=jax.ShapeDtypeStruct((TOKENS, D_MODEL), F32),
        scratch_shapes=[pltpu.VMEM((k, tn), BF16)],
        compiler_params=_params("arbitrary", "arbitrary"),
        name="residual_matmul",
    )(x, w, h)


CARRY_ROWS = 8


def _ffn_in_kernel(x_ref, wg_ref, wu_ref, cw_ref, cb_ref, o_ref,
                   wgb_ref, wub_ref, gs_ref, *, tm):
    m = pl.program_id(1)

    @pl.when(m == 0)
    def _():
        wgb_ref[...] = wg_ref[...].astype(BF16)
        wub_ref[...] = wu_ref[...].astype(BF16)

    @pl.when(m % (SEQ // tm) == 0)
    def _():
        gs_ref[0:CARRY_ROWS, :] = jnp.zeros((CARRY_ROWS, gs_ref.shape[1]), F32)

    x = x_ref[...]
    g = jnp.dot(x, wgb_ref[...], preferred_element_type=F32)
    u = jnp.dot(x, wub_ref[...], preferred_element_type=F32)
    gs_ref[CARRY_ROWS:CARRY_ROWS + tm, :] = g
    g1 = gs_ref[CARRY_ROWS - 1:CARRY_ROWS - 1 + tm, :]
    g2 = gs_ref[CARRY_ROWS - 2:CARRY_ROWS - 2 + tm, :]
    conv = (cw_ref[2:3, :] * g + cw_ref[1:2, :] * g1 + cw_ref[0:1, :] * g2
            + cb_ref[...])
    gelu = 0.5 * conv * (1.0 + lax.erf(conv * (2.0 ** -0.5)))
    o_ref[...] = (gelu * u).astype(BF16)
    gs_ref[0:CARRY_ROWS, :] = gs_ref[tm:tm + CARRY_ROWS, :]


def _ffn_in(a, w_in, layer, conv_w, conv_b):
    tm, tn = 512, 512
    nt = D_FF // tn
    return pl.pallas_call(
        functools.partial(_ffn_in_kernel, tm=tm),
        grid=(nt, TOKENS // tm),
        in_specs=[pl.BlockSpec((tm, D_MODEL), lambda n, m: (m, 0)),
                  pl.BlockSpec((None, D_MODEL, tn), lambda n, m: (layer, 0, n)),
                  pl.BlockSpec((None, D_MODEL, tn), lambda n, m: (layer, 0, nt + n)),
                  pl.BlockSpec((CONV_WIDTH, tn), lambda n, m: (0, n)),
                  pl.BlockSpec((1, tn), lambda n, m: (0, n))],
        out_specs=pl.BlockSpec((tm, tn), lambda n, m: (m, n)),
        out_shape=jax.ShapeDtypeStruct((TOKENS, D_FF), BF16),
        scratch_shapes=[pltpu.VMEM((D_MODEL, tn), BF16),
                        pltpu.VMEM((D_MODEL, tn), BF16),
                        pltpu.VMEM((tm + CARRY_ROWS, tn), F32)],
        compiler_params=_params("arbitrary", "arbitrary"),
        name="ffn_in",
    )(a, w_in, w_in, conv_w, conv_b.reshape(1, D_FF))


def _split3(x):
    hi = x.astype(BF16)
    r1 = x - hi.astype(F32)
    mid = r1.astype(BF16)
    lo = (r1 - mid.astype(F32)).astype(BF16)
    return jnp.concatenate([hi, mid, lo], axis=-1)


CUM_BLOCK = 512


def _fox_decay_kernel(x_ref, w_ref, b_ref, selq_ref, selk_ref, oneq_ref, onek_ref,
                      cq_ref, ck_ref, c_ref):
    fl = jnp.dot(x_ref[...], w_ref[...], preferred_element_type=F32)
    z = fl + b_ref[...]
    logf = (jnp.minimum(z, 0.0) - jnp.log1p(jnp.exp(-jnp.abs(z)))) * LOG2E
    row = lax.broadcasted_iota(jnp.int32, (CUM_BLOCK, CUM_BLOCK), 0)
    col = lax.broadcasted_iota(jnp.int32, (CUM_BLOCK, CUM_BLOCK), 1)
    tri = jnp.where(row >= col, 1.0, 0.0).astype(BF16)
    carry = jnp.zeros((1, LANES), F32)
    for blk in range(SEQ // CUM_BLOCK):
        sl = slice(blk * CUM_BLOCK, (blk + 1) * CUM_BLOCK)
        cs = jnp.dot(tri, _split3(logf[sl]), preferred_element_type=F32)
        c_blk = (cs[:, :LANES] + cs[:, LANES:2 * LANES] + cs[:, 2 * LANES:]) + carry
        carry = c_blk[CUM_BLOCK - 1:CUM_BLOCK, :]
        c_ref[sl, :] = c_blk
    parts = _split3(c_ref[...])
    cq = jnp.dot(parts, selq_ref[...], preferred_element_type=F32) + oneq_ref[...]
    ck = jnp.dot(parts, selk_ref[...], preferred_element_type=F32) + onek_ref[...]
    cq_ref[...] = cq.astype(BF16)
    ck_ref[...] = ck.astype(BF16)


def _decay_selectors():
    selq = np.zeros((3 * LANES, D_MODEL), np.float32)
    selk = np.zeros((3 * LANES, D_MODEL), np.float32)
    oneq = np.zeros((1, D_MODEL), np.float32)
    onek = np.zeros((1, D_MODEL), np.float32)
    for h in range(FOX_HEADS):
        for p in range(3):
            selq[p * LANES + h, h * HEAD_DIM + p] = 1.0
            selk[p * LANES + h, h * HEAD_DIM + 3 + p] = -1.0
            oneq[0, h * HEAD_DIM + 3 + p] = 1.0
            onek[0, h * HEAD_DIM + p] = 1.0
    return (jnp.asarray(selq, BF16), jnp.asarray(selk, BF16),
            jnp.asarray(oneq), jnp.asarray(onek))


def _fox_decay(a, w_f, b_f):
    pad = LANES - FOX_HEADS
    w_pad = jnp.pad(w_f, ((0, 0), (0, pad))).astype(BF16)
    b_pad = jnp.pad(b_f, (0, pad)).reshape(1, LANES)
    selq, selk, oneq, onek = _decay_selectors()
    full = lambda shape: pl.BlockSpec(shape, lambda b: (0, 0))
    out = jax.ShapeDtypeStruct((TOKENS, D_MODEL), BF16)
    return pl.pallas_call(
        _fox_decay_kernel,
        grid=(BATCH,),
        in_specs=[pl.BlockSpec((SEQ, D_MODEL), lambda b: (b, 0)),
                  full((D_MODEL, LANES)), full((1, LANES)),
                  full((3 * LANES, D_MODEL)), full((3 * LANES, D_MODEL)),
                  full((1, D_MODEL)), full((1, D_MODEL))],
        out_specs=[pl.BlockSpec((SEQ, D_MODEL), lambda b: (b, 0)),
                   pl.BlockSpec((SEQ, D_MODEL), lambda b: (b, 0))],
        out_shape=[out, out],
        scratch_shapes=[pltpu.VMEM((SEQ, LANES), F32)],
        compiler_params=_params("arbitrary"),
        name="fox_decay",
    )(a, w_pad, b_pad, selq, selk, oneq, onek)


def _softmax_step(s, v, carry):
    m, l, acc = carry
    m_new = jnp.maximum(m, jnp.max(s, axis=-1, keepdims=True))
    alpha = jnp.exp2(m - m_new)
    p = jnp.exp2(s - m_new)
    l = alpha * l + jnp.sum(p, axis=-1, keepdims=True)
    acc = alpha * acc + jnp.dot(p.astype(BF16), v, preferred_element_type=F32)
    return m_new, l, acc


def _softmax_init(width):
    return (jnp.full((TQ, 1), NEG, F32), jnp.zeros((TQ, 1), F32),
            jnp.zeros((TQ, width), F32))


_NT = (((1,), (1,)), ((), ()))


def _fox_attn_kernel(q_ref, cq_ref, k_ref, ck_ref, v_ref, g_ref, o_ref):
    row = lax.broadcasted_iota(jnp.int32, (TQ, TK), 0)
    col = lax.broadcasted_iota(jnp.int32, (TQ, TK), 1)
    causal = col <= row

    def q_block(i, _):
        qs = pl.multiple_of(i * TQ, TQ)
        qa = jnp.concatenate([q_ref[0, pl.ds(qs, TQ), :],
                              cq_ref[0, pl.ds(qs, TQ), :]], axis=-1)

        def scores(j):
            ks = pl.multiple_of(j * TK, TK)
            ka = jnp.concatenate([k_ref[0, pl.ds(ks, TK), :],
                                  ck_ref[0, pl.ds(ks, TK), :]], axis=-1)
            s = lax.dot_general(qa, ka, _NT, preferred_element_type=F32)
            return s, v_ref[0, pl.ds(ks, TK), :]

        def far(j, carry):
            s, v = scores(j)
            return _softmax_step(s, v, carry)

        carry = lax.fori_loop(0, i, far, _softmax_init(HEAD_DIM))
        s, v = scores(i)
        _, l, acc = _softmax_step(jnp.where(causal, s, NEG), v, carry)
        gate = g_ref[0, pl.ds(qs, TQ), :].astype(F32)
        o_ref[0, pl.ds(qs, TQ), :] = (acc / l * gate).astype(BF16)
        return 0

    lax.fori_loop(0, NQ, q_block, 0)


def _fox_attention(proj, cq, ck):
    blk = lambda off: pl.BlockSpec((1, SEQ, HEAD_DIM), lambda b, h: (b, 0, off + h))
    return pl.pallas_call(
        _fox_attn_kernel,
        grid=(BATCH, FOX_HEADS),
        in_specs=[blk(0), blk(0), blk(FOX_HEADS), blk(0), blk(2 * FOX_HEADS),
                  blk(3 * FOX_HEADS)],
        out_specs=blk(0),
        out_shape=jax.ShapeDtypeStruct((BATCH, SEQ, D_MODEL), BF16),
        compiler_params=_params("arbitrary", "arbitrary"),
        name="fox_attn",
    )(proj, cq, proj, ck, proj, proj)


def _bias_tile_kernel(rb_ref, o_ref):
    mp = pl.program_id(0)
    row = lax.broadcasted_iota(jnp.int32, (TQ, TK), 0)
    col = lax.broadcasted_iota(jnp.int32, (TQ, TK), 1)
    max_exact = NUM_BUCKETS // 2
    for t in range(2):
        n = jnp.maximum(row - col + t * TK, 0)
        nf = jnp.maximum(n, 1).astype(F32)
        large = max_exact + (jnp.log(nf / max_exact) / math.log(MAX_DISTANCE / max_exact)
                             * (NUM_BUCKETS - max_exact)).astype(jnp.int32)
        large = jnp.minimum(large, NUM_BUCKETS - 1)
        bucket = jnp.where(n < max_exact, n, large)
        val = jnp.zeros((TQ, TK), F32)
        for bkt in range(NUM_BUCKETS):
            val = jnp.where(bucket == bkt, rb_ref[bkt, mp], val)
        val = (val - rb_ref[NUM_BUCKETS - 1, mp]) * LOG2E
        if t == 0:
            val = jnp.where(col <= row, val, NEG)
        o_ref[0, t] = val


def _bias_tiles(rel_bias):
    n_maps = 2 * DIFF_HEADS
    return pl.pallas_call(
        _bias_tile_kernel,
        grid=(n_maps,),
        in_specs=[pl.BlockSpec(memory_space=pltpu.SMEM)],
        out_specs=pl.BlockSpec((1, 2, TQ, TK), lambda mp: (mp, 0, 0, 0)),
        out_shape=jax.ShapeDtypeStruct((n_maps, 2, TQ, TK), F32),
        compiler_params=_params("arbitrary"),
        name="bias_tiles",
    )(rel_bias.reshape(NUM_BUCKETS, n_maps))


def _diff_attn_kernel(q1_ref, q2_ref, k1_ref, k2_ref, v_ref, bias_ref, lam_ref,
                      sub_ref, o_ref, *, lambda_init):
    lamf = lam_ref[...]
    lam_full = (jnp.exp(jnp.sum(lamf[0:1] * lamf[1:2], axis=-1, keepdims=True))
                - jnp.exp(jnp.sum(lamf[2:3] * lamf[3:4], axis=-1, keepdims=True))
                + lambda_init)
    width = 2 * HEAD_DIM

    def q_block(i, _):
        qs = pl.multiple_of(i * TQ, TQ)
        outs = []
        for mp, (q_ref, k_ref) in enumerate(((q1_ref, k1_ref), (q2_ref, k2_ref))):
            q = q_ref[0, pl.ds(qs, TQ), :]

            def scores(j, k_ref=k_ref, q=q):
                ks = pl.multiple_of(j * TK, TK)
                s = lax.dot_general(q, k_ref[0, pl.ds(ks, TK), :], _NT,
                                    preferred_element_type=F32)
                return s, v_ref[0, pl.ds(ks, TK), :]

            s, v = scores(jnp.maximum(i - 1, 0))
            s = s + bias_ref[mp, 1] + jnp.where(i == 0, NEG, 0.0).astype(F32)
            carry = _softmax_step(s, v, _softmax_init(width))

            def far(j, carry, scores=scores):
                s, v = scores(j)
                return _softmax_step(s, v, carry)

            carry = lax.fori_loop(0, i - 1, far, carry)
            s, v = scores(i)
            _, l, acc = _softmax_step(s + bias_ref[mp, 0], v, carry)
            outs.append(acc / l)
        o = outs[0] - lam_full * outs[1]
        ms = jnp.mean(o * o, axis=-1, keepdims=True)
        o = o * lax.rsqrt(ms + SUBLN_EPS) * sub_ref[...] * (1.0 - lambda_init)
        o_ref[0, pl.ds(qs, TQ), :] = o.astype(BF16)
        return 0

    lax.fori_loop(0, NQ, q_block, 0)


def _diff_attention(proj, bias_tiles, lam, subln, lambda_init):
    qk = lambda off, j: pl.BlockSpec((1, SEQ, HEAD_DIM),
                                     lambda h, b: (b, 0, off + 2 * h + j))
    n_maps = 2 * DIFF_HEADS
    width = 2 * HEAD_DIM
    return pl.pallas_call(
        functools.partial(_diff_attn_kernel, lambda_init=lambda_init),
        grid=(DIFF_HEADS, BATCH),
        in_specs=[qk(0, 0), qk(0, 1), qk(n_maps, 0), qk(n_maps, 1),
                  pl.BlockSpec((1, SEQ, width), lambda h, b: (b, 0, 2 * DIFF_HEADS + h)),
                  pl.BlockSpec((2, 2, TQ, TK), lambda h, b: (h, 0, 0, 0)),
                  pl.BlockSpec((4, HEAD_DIM), lambda h, b: (0, 0)),
                  pl.BlockSpec((1, width), lambda h, b: (0, 0))],
        out_specs=pl.BlockSpec((1, SEQ, width), lambda h, b: (b, 0, h)),
        out_shape=jax.ShapeDtypeStruct((BATCH, SEQ, D_MODEL), BF16),
        compiler_params=_params("arbitrary", "arbitrary"),
        name="diff_attn",
    )(proj, proj, proj, proj, proj, bias_tiles, lam, subln.reshape(1, width))


def kernel(x, attn_norm, ffn_norm, final_norm, fox_w_in, fox_b_f, fox_qk_norm, fox_w_out,
           diff_w_in, diff_lambda, diff_subln, diff_w_out, rel_bias,
           ffn_w_in, ffn_conv_w, ffn_conv_b, ffn_w_out):
    h = x.reshape(TOKENS, D_MODEL)
    bias_tiles = _bias_tiles(rel_bias)
    for i in range(DEPTH):
        j = i // 2
        a = _rmsnorm(h, attn_norm[i], BF16)
        if i % 2 == 0:
            proj = _fox_proj(a, fox_w_in, j, fox_qk_norm[j])
            cq, ck = _fox_decay(a, fox_w_in[j, :, 4 * D_MODEL:], fox_b_f[j])
            o = _fox_attention(proj.reshape(BATCH, SEQ, 4 * D_MODEL),
                               cq.reshape(BATCH, SEQ, D_MODEL),
                               ck.reshape(BATCH, SEQ, D_MODEL))
            h = _residual_matmul(o.reshape(TOKENS, D_MODEL), fox_w_out, j, h, 512, 1024)
        else:
            lambda_init = 0.8 - 0.6 * math.exp(-0.3 * i)
            proj = _diff_proj(a, diff_w_in, j)
            o = _diff_attention(proj.reshape(BATCH, SEQ, 3 * D_MODEL), bias_tiles,
                                diff_lambda[j], diff_subln[j], lambda_init)
            h = _residual_matmul(o.reshape(TOKENS, D_MODEL), diff_w_out, j, h, 512, 1024)
        a = _rmsnorm(h, ffn_norm[i], BF16)
        y = _ffn_in(a, ffn_w_in, i, ffn_conv_w[i], ffn_conv_b[i])
        h = _residual_matmul(y, ffn_w_out, i, h, 512, 512)
    return _rmsnorm(h, final_norm, F32).reshape(BATCH, SEQ, D_MODEL)
```

```python
import functools
import math

import jax
import jax.numpy as jnp
import numpy as np
from jax import lax
from jax.experimental import pallas as pl
from jax.experimental.pallas import tpu as pltpu

D_MODEL = 2048
BATCH = 4
SEQ = 2048
TOKENS = BATCH * SEQ
DEPTH = 4
HEAD_DIM = 128
FOX_HEADS = D_MODEL // HEAD_DIM
DIFF_HEADS = D_MODEL // (2 * HEAD_DIM)
D_FF = 5632
CONV_WIDTH = 3
NUM_BUCKETS = 32
MAX_DISTANCE = 128
NORM_EPS = 1e-6
SUBLN_EPS = 1e-5

LANES = 128
VMEM_LIMIT = 56 * 1024 * 1024
NEG = -1e30
LOG2E = math.log2(math.e)
Q_SCALE = HEAD_DIM ** -0.5 * LOG2E

TQ = 256
TK = 256
NQ = SEQ // TQ

F32 = jnp.float32
BF16 = jnp.bfloat16


def _params(*sem):
    return pltpu.CompilerParams(dimension_semantics=sem, vmem_limit_bytes=VMEM_LIMIT)


def _rmsnorm_kernel(x_ref, g_ref, o_ref):
    x = x_ref[...]
    ms = jnp.mean(x * x, axis=-1, keepdims=True)
    o_ref[...] = (x * lax.rsqrt(ms + NORM_EPS) * g_ref[...]).astype(o_ref.dtype)


def _rmsnorm(h, g, out_dtype):
    tm = 512
    return pl.pallas_call(
        _rmsnorm_kernel,
        grid=(TOKENS // tm,),
        in_specs=[pl.BlockSpec((tm, D_MODEL), lambda m: (m, 0)),
                  pl.BlockSpec((1, D_MODEL), lambda m: (0, 0))],
        out_specs=pl.BlockSpec((tm, D_MODEL), lambda m: (m, 0)),
        out_shape=jax.ShapeDtypeStruct((TOKENS, D_MODEL), out_dtype),
        compiler_params=_params("arbitrary"),
        name="rmsnorm",
    )(h, g.reshape(1, D_MODEL))


def _head_rmsnorm(acc, gain, mult):
    ms = jnp.mean(acc * acc, axis=-1, keepdims=True)
    return acc * lax.rsqrt(ms + NORM_EPS) * (gain * mult)


def _fox_proj_kernel(x_ref, w_ref, qkn_ref, o_ref, wb_ref, *, tn):
    n = pl.program_id(0)

    @pl.when(pl.program_id(1) == 0)
    def _():
        wb_ref[...] = w_ref[...].astype(BF16)

    acc = jnp.dot(x_ref[...], wb_ref[...], preferred_element_type=F32)
    per = D_MODEL // tn

    def normed(row, mult):
        gain = qkn_ref[row:row + 1, :]
        for c in range(tn // HEAD_DIM):
            sl = slice(c * HEAD_DIM, (c + 1) * HEAD_DIM)
            o_ref[:, sl] = _head_rmsnorm(acc[:, sl], gain, mult).astype(BF16)

    @pl.when(n < per)
    def _():
        normed(0, Q_SCALE)

    @pl.when((n >= per) & (n < 2 * per))
    def _():
        normed(1, 1.0)

    @pl.when((n >= 2 * per) & (n < 3 * per))
    def _():
        o_ref[...] = acc.astype(BF16)

    @pl.when(n >= 3 * per)
    def _():
        o_ref[...] = jax.nn.sigmoid(acc).astype(BF16)


def _fox_proj(a, w_in, layer, qk_norm):
    tm, tn = 512, 1024
    n_out = 4 * D_MODEL
    return pl.pallas_call(
        functools.partial(_fox_proj_kernel, tn=tn),
        grid=(n_out // tn, TOKENS // tm),
        in_specs=[pl.BlockSpec((tm, D_MODEL), lambda n, m: (m, 0)),
                  pl.BlockSpec((None, D_MODEL, tn), lambda n, m: (layer, 0, n)),
                  pl.BlockSpec((2, HEAD_DIM), lambda n, m: (0, 0))],
        out_specs=pl.BlockSpec((tm, tn), lambda n, m: (m, n)),
        out_shape=jax.ShapeDtypeStruct((TOKENS, n_out), BF16),
        scratch_shapes=[pltpu.VMEM((D_MODEL, tn), BF16)],
        compiler_params=_params("arbitrary", "arbitrary"),
        name="fox_proj",
    )(a, w_in, qk_norm)


def _diff_proj_kernel(x_ref, w_ref, o_ref, wb_ref, *, tn):
    n = pl.program_id(0)

    @pl.when(pl.program_id(1) == 0)
    def _():
        wb_ref[...] = w_ref[...].astype(BF16)

    acc = jnp.dot(x_ref[...], wb_ref[...], preferred_element_type=F32)
    mult = jnp.where(n < D_MODEL // tn, Q_SCALE, 1.0).astype(F32)
    o_ref[...] = (acc * mult).astype(BF16)


def _diff_proj(a, w_in, layer):
    tm, tn = 512, 1024
    n_out = 3 * D_MODEL
    return pl.pallas_call(
        functools.partial(_diff_proj_kernel, tn=tn),
        grid=(n_out // tn, TOKENS // tm),
        in_specs=[pl.BlockSpec((tm, D_MODEL), lambda n, m: (m, 0)),
                  pl.BlockSpec((None, D_MODEL, tn), lambda n, m: (layer, 0, n))],
        out_specs=pl.BlockSpec((tm, tn), lambda n, m: (m, n)),
        out_shape=jax.ShapeDtypeStruct((TOKENS, n_out), BF16),
        scratch_shapes=[pltpu.VMEM((D_MODEL, tn), BF16)],
        compiler_params=_params("arbitrary", "arbitrary"),
        name="diff_proj",
    )(a, w_in)


def _residual_matmul_kernel(x_ref, w_ref, h_ref, o_ref, wb_ref):
    @pl.when(pl.program_id(1) == 0)
    def _():
        wb_ref[...] = w_ref[...].astype(BF16)

    acc = jnp.dot(x_ref[...], wb_ref[...], preferred_element_type=F32)
    o_ref[...] = h_ref[...] + acc


def _residual_matmul(x, w, layer, h, tm, tn):
    k = x.shape[1]
    return pl.pallas_call(
        _residual_matmul_kernel,
        grid=(D_MODEL // tn, TOKENS // tm),
        in_specs=[pl.BlockSpec((tm, k), lambda n, m: (m, 0)),
                  pl.BlockSpec((None, k, tn), lambda n, m: (layer, 0, n)),
                  pl.BlockSpec((tm, tn), lambda n, m: (m, n))],
        out_specs=pl.BlockSpec((tm, tn), lambda n, m: (m, n)),
        out_shape=jax.ShapeDtypeStruct((TOKENS, D_MODEL), F32),
        scratch_shapes=[pltpu.VMEM((k, tn), BF16)],
        compiler_params=_params("arbitrary", "arbitrary"),
        name="residual_matmul",
    )(x, w, h)


CARRY_ROWS = 8


def _ffn_in_kernel(x_ref, wg_ref, wu_ref, cw_ref, cb_ref, o_ref,
                   wgb_ref, wub_ref, gs_ref, *, tm):
    m = pl.program_id(1)

    @pl.when(m == 0)
    def _():
        wgb_ref[...] = wg_ref[...].astype(BF16)
        wub_ref[...] = wu_ref[...].astype(BF16)

    @pl.when(m % (SEQ // tm) == 0)
    def _():
        gs_ref[0:CARRY_ROWS, :] = jnp.zeros((CARRY_ROWS, gs_ref.shape[1]), F32)

    x = x_ref[...]
    g = jnp.dot(x, wgb_ref[...], preferred_element_type=F32)
    u = jnp.dot(x, wub_ref[...], preferred_element_type=F32)
    gs_ref[CARRY_ROWS:CARRY_ROWS + tm, :] = g
    g1 = gs_ref[CARRY_ROWS - 1:CARRY_ROWS - 1 + tm, :]
    g2 = gs_ref[CARRY_ROWS - 2:CARRY_ROWS - 2 + tm, :]
    conv = (cw_ref[2:3, :] * g + cw_ref[1:2, :] * g1 + cw_ref[0:1, :] * g2
            + cb_ref[...])
    gelu = 0.5 * conv * (1.0 + lax.erf(conv * (2.0 ** -0.5)))
    o_ref[...] = (gelu * u).astype(BF16)
    gs_ref[0:CARRY_ROWS, :] = gs_ref[tm:tm + CARRY_ROWS, :]


def _ffn_in(a, w_in, layer, conv_w, conv_b):
    tm, tn = 512, 512
    nt = D_FF // tn
    return pl.pallas_call(
        functools.partial(_ffn_in_kernel, tm=tm),
        grid=(nt, TOKENS // tm),
        in_specs=[pl.BlockSpec((tm, D_MODEL), lambda n, m: (m, 0)),
                  pl.BlockSpec((None, D_MODEL, tn), lambda n, m: (layer, 0, n)),
                  pl.BlockSpec((None, D_MODEL, tn), lambda n, m: (layer, 0, nt + n)),
                  pl.BlockSpec((CONV_WIDTH, tn), lambda n, m: (0, n)),
                  pl.BlockSpec((1, tn), lambda n, m: (0, n))],
        out_specs=pl.BlockSpec((tm, tn), lambda n, m: (m, n)),
        out_shape=jax.ShapeDtypeStruct((TOKENS, D_FF), BF16),
        scratch_shapes=[pltpu.VMEM((D_MODEL, tn), BF16),
                        pltpu.VMEM((D_MODEL, tn), BF16),
                        pltpu.VMEM((tm + CARRY_ROWS, tn), F32)],
        compiler_params=_params("arbitrary", "arbitrary"),
        name="ffn_in",
    )(a, w_in, w_in, conv_w, conv_b.reshape(1, D_FF))


def _split3(x):
    hi = x.astype(BF16)
    r1 = x - hi.astype(F32)
    mid = r1.astype(BF16)
    lo = (r1 - mid.astype(F32)).astype(BF16)
    return jnp.concatenate([hi, mid, lo], axis=-1)


CUM_BLOCK = 512


def _fox_decay_kernel(x_ref, w_ref, b_ref, selq_ref, selk_ref, oneq_ref, onek_ref,
                      cq_ref, ck_ref, c_ref):
    fl = jnp.dot(x_ref[...], w_ref[...], preferred_element_type=F32)
    z = fl + b_ref[...]
    logf = (jnp.minimum(z, 0.0) - jnp.log1p(jnp.exp(-jnp.abs(z)))) * LOG2E
    row = lax.broadcasted_iota(jnp.int32, (CUM_BLOCK, CUM_BLOCK), 0)
    col = lax.broadcasted_iota(jnp.int32, (CUM_BLOCK, CUM_BLOCK), 1)
    tri = jnp.where(row >= col, 1.0, 0.0).astype(BF16)
    carry = jnp.zeros((1, LANES), F32)
    for blk in range(SEQ // CUM_BLOCK):
        sl = slice(blk * CUM_BLOCK, (blk + 1) * CUM_BLOCK)
        cs = jnp.dot(tri, _split3(logf[sl]), preferred_element_type=F32)
        c_blk = (cs[:, :LANES] + cs[:, LANES:2 * LANES] + cs[:, 2 * LANES:]) + carry
        carry = c_blk[CUM_BLOCK - 1:CUM_BLOCK, :]
        c_ref[sl, :] = c_blk
    parts = _split3(c_ref[...])
    cq = jnp.dot(parts, selq_ref[...], preferred_element_type=F32) + oneq_ref[...]
    ck = jnp.dot(parts, selk_ref[...], preferred_element_type=F32) + onek_ref[...]
    cq_ref[...] = cq.astype(BF16)
    ck_ref[...] = ck.astype(BF16)


def _decay_selectors():
    selq = np.zeros((3 * LANES, D_MODEL), np.float32)
    selk = np.zeros((3 * LANES, D_MODEL), np.float32)
    oneq = np.zeros((1, D_MODEL), np.float32)
    onek = np.zeros((1, D_MODEL), np.float32)
    for h in range(FOX_HEADS):
        for p in range(3):
            selq[p * LANES + h, h * HEAD_DIM + p] = 1.0
            selk[p * LANES + h, h * HEAD_DIM + 3 + p] = -1.0
            oneq[0, h * HEAD_DIM + 3 + p] = 1.0
            onek[0, h * HEAD_DIM + p] = 1.0
    return (jnp.asarray(selq, BF16), jnp.asarray(selk, BF16),
            jnp.asarray(oneq), jnp.asarray(onek))


def _fox_decay(a, w_f, b_f):
    pad = LANES - FOX_HEADS
    w_pad = jnp.pad(w_f, ((0, 0), (0, pad))).astype(BF16)
    b_pad = jnp.pad(b_f, (0, pad)).reshape(1, LANES)
    selq, selk, oneq, onek = _decay_selectors()
    full = lambda shape: pl.BlockSpec(shape, lambda b: (0, 0))
    out = jax.ShapeDtypeStruct((TOKENS, D_MODEL), BF16)
    return pl.pallas_call(
        _fox_decay_kernel,
        grid=(BATCH,),
        in_specs=[pl.BlockSpec((SEQ, D_MODEL), lambda b: (b, 0)),
                  full((D_MODEL, LANES)), full((1, LANES)),
                  full((3 * LANES, D_MODEL)), full((3 * LANES, D_MODEL)),
                  full((1, D_MODEL)), full((1, D_MODEL))],
        out_specs=[pl.BlockSpec((SEQ, D_MODEL), lambda b: (b, 0)),
                   pl.BlockSpec((SEQ, D_MODEL), lambda b: (b, 0))],
        out_shape=[out, out],
        scratch_shapes=[pltpu.VMEM((SEQ, LANES), F32)],
        compiler_params=_params("arbitrary"),
        name="fox_decay",
    )(a, w_pad, b_pad, selq, selk, oneq, onek)


_NT = (((1,), (1,)), ((), ()))


def _lane_fold(x, op):
    out = x[:, :LANES]
    for c in range(1, x.shape[1] // LANES):
        out = op(out, x[:, c * LANES:(c + 1) * LANES])
    return out


def _row_max(s_ref, n_tiles):
    part = _lane_fold(s_ref[:, :TK], jnp.maximum)
    for j in range(1, n_tiles):
        part = jnp.maximum(part, _lane_fold(s_ref[:, j * TK:(j + 1) * TK], jnp.maximum))
    return jnp.max(part, axis=-1, keepdims=True)


def _exp_pass(s_ref, p_ref, m, n_tiles):
    part = None
    for j in range(n_tiles):
        sl = slice(j * TK, (j + 1) * TK)
        p = jnp.exp2(s_ref[:, sl] - m)
        folded = _lane_fold(p, jnp.add)
        part = folded if part is None else part + folded
        p_ref[:, sl] = p.astype(p_ref.dtype)
    return jnp.sum(part, axis=-1, keepdims=True)


def _fox_attn_kernel(q_ref, cq_ref, k_ref, ck_ref, v_ref, g_ref, o_ref, s_ref, p_ref):
    row = lax.broadcasted_iota(jnp.int32, (TQ, TK), 0)
    col = lax.broadcasted_iota(jnp.int32, (TQ, TK), 1)
    causal = col <= row
    for i in range(NQ):
        slot = i % 2
        qs = i * TQ
        keys = qs + TK
        sr, pr = s_ref.at[slot], p_ref.at[slot]
        qa = jnp.concatenate([q_ref[0, qs:qs + TQ, :], cq_ref[0, qs:qs + TQ, :]], axis=-1)
        if i > 0:
            ka = jnp.concatenate([k_ref[0, :qs, :], ck_ref[0, :qs, :]], axis=-1)
            sr[:, :qs] = lax.dot_general(qa, ka, _NT, preferred_element_type=F32)
        ka = jnp.concatenate([k_ref[0, qs:keys, :], ck_ref[0, qs:keys, :]], axis=-1)
        s_diag = lax.dot_general(qa, ka, _NT, preferred_element_type=F32)
        sr[:, qs:keys] = jnp.where(causal, s_diag, NEG)
        m = _row_max(sr, i + 1)
        l = _exp_pass(sr, pr, m, i + 1)
        acc = jnp.dot(pr[:, :keys], v_ref[0, :keys, :], preferred_element_type=F32)
        gate = g_ref[0, qs:qs + TQ, :].astype(F32)
        o_ref[0, qs:qs + TQ, :] = (acc * (1.0 / l) * gate).astype(BF16)


def _fox_attention(proj, cq, ck):
    blk = lambda off: pl.BlockSpec((1, SEQ, HEAD_DIM), lambda b, h: (b, 0, off + h))
    return pl.pallas_call(
        _fox_attn_kernel,
        grid=(BATCH, FOX_HEADS),
        in_specs=[blk(0), blk(0), blk(FOX_HEADS), blk(0), blk(2 * FOX_HEADS),
                  blk(3 * FOX_HEADS)],
        out_specs=blk(0),
        out_shape=jax.ShapeDtypeStruct((BATCH, SEQ, D_MODEL), BF16),
        scratch_shapes=[pltpu.VMEM((2, TQ, SEQ), F32), pltpu.VMEM((2, TQ, SEQ), BF16)],
        compiler_params=_params("arbitrary", "arbitrary"),
        name="fox_attn",
    )(proj, cq, proj, ck, proj, proj)


def _bias_tile_kernel(rb_ref, o_ref):
    mp = pl.program_id(0)
    row = lax.broadcasted_iota(jnp.int32, (TQ, TK), 0)
    col = lax.broadcasted_iota(jnp.int32, (TQ, TK), 1)
    max_exact = NUM_BUCKETS // 2
    for t in range(2):
        n = jnp.maximum(row - col + t * TK, 0)
        nf = jnp.maximum(n, 1).astype(F32)
        large = max_exact + (jnp.log(nf / max_exact) / math.log(MAX_DISTANCE / max_exact)
                             * (NUM_BUCKETS - max_exact)).astype(jnp.int32)
        large = jnp.minimum(large, NUM_BUCKETS - 1)
        bucket = jnp.where(n < max_exact, n, large)
        val = jnp.zeros((TQ, TK), F32)
        for bkt in range(NUM_BUCKETS):
            val = jnp.where(bucket == bkt, rb_ref[bkt, mp], val)
        val = (val - rb_ref[NUM_BUCKETS - 1, mp]) * LOG2E
        if t == 0:
            val = jnp.where(col <= row, val, NEG)
        o_ref[0, t] = val


def _bias_tiles(rel_bias):
    n_maps = 2 * DIFF_HEADS
    return pl.pallas_call(
        _bias_tile_kernel,
        grid=(n_maps,),
        in_specs=[pl.BlockSpec(memory_space=pltpu.SMEM)],
        out_specs=pl.BlockSpec((1, 2, TQ, TK), lambda mp: (mp, 0, 0, 0)),
        out_shape=jax.ShapeDtypeStruct((n_maps, 2, TQ, TK), F32),
        compiler_params=_params("arbitrary"),
        name="bias_tiles",
    )(rel_bias.reshape(NUM_BUCKETS, n_maps))


def _diff_attn_kernel(q1_ref, q2_ref, k1_ref, k2_ref, v_ref, bias_ref, lam_ref,
                      sub_ref, o_ref, s_ref, p_ref, *, lambda_init):
    lamf = lam_ref[...]
    lam_full = (jnp.exp(jnp.sum(lamf[0:1] * lamf[1:2], axis=-1, keepdims=True))
                - jnp.exp(jnp.sum(lamf[2:3] * lamf[3:4], axis=-1, keepdims=True))
                + lambda_init)
    for i in range(NQ):
        qs = i * TQ
        keys = qs + TK
        near = qs - TK
        outs = []
        for mp, (q_ref, k_ref) in enumerate(((q1_ref, k1_ref), (q2_ref, k2_ref))):
            slot = 2 * (i % 2) + mp
            sr, pr = s_ref.at[slot], p_ref.at[slot]
            q = q_ref[0, qs:qs + TQ, :]
            if i > 1:
                sr[:, :near] = lax.dot_general(q, k_ref[0, :near, :], _NT,
                                               preferred_element_type=F32)
            if i > 0:
                sr[:, near:qs] = lax.dot_general(q, k_ref[0, near:qs, :], _NT,
                                                 preferred_element_type=F32) + bias_ref[mp, 1]
            sr[:, qs:keys] = lax.dot_general(q, k_ref[0, qs:keys, :], _NT,
                                             preferred_element_type=F32) + bias_ref[mp, 0]
            m = _row_max(sr, i + 1)
            l = _exp_pass(sr, pr, m, i + 1)
            acc = jnp.dot(pr[:, :keys], v_ref[0, :keys, :], preferred_element_type=F32)
            outs.append(acc * (1.0 / l))
        o = outs[0] - lam_full * outs[1]
        ms = jnp.mean(o * o, axis=-1, keepdims=True)
        o = o * lax.rsqrt(ms + SUBLN_EPS) * sub_ref[...] * (1.0 - lambda_init)
        o_ref[0, qs:qs + TQ, :] = o.astype(BF16)


def _diff_attention(proj, bias_tiles, lam, subln, lambda_init):
    qk = lambda off, j: pl.BlockSpec((1, SEQ, HEAD_DIM),
                                     lambda h, b: (b, 0, off + 2 * h + j))
    n_maps = 2 * DIFF_HEADS
    width = 2 * HEAD_DIM
    return pl.pallas_call(
        functools.partial(_diff_attn_kernel, lambda_init=lambda_init),
        grid=(DIFF_HEADS, BATCH),
        in_specs=[qk(0, 0), qk(0, 1), qk(n_maps, 0), qk(n_maps, 1),
                  pl.BlockSpec((1, SEQ, width), lambda h, b: (b, 0, 2 * DIFF_HEADS + h)),
                  pl.BlockSpec((2, 2, TQ, TK), lambda h, b: (h, 0, 0, 0)),
                  pl.BlockSpec((4, HEAD_DIM), lambda h, b: (0, 0)),
                  pl.BlockSpec((1, width), lambda h, b: (0, 0))],
        out_specs=pl.BlockSpec((1, SEQ, width), lambda h, b: (b, 0, h)),
        out_shape=jax.ShapeDtypeStruct((BATCH, SEQ, D_MODEL), BF16),
        scratch_shapes=[pltpu.VMEM((4, TQ, SEQ), F32), pltpu.VMEM((4, TQ, SEQ), BF16)],
        compiler_params=_params("arbitrary", "arbitrary"),
        name="diff_attn",
    )(proj, proj, proj, proj, proj, bias_tiles, lam, subln.reshape(1, width))


def kernel(x, attn_norm, ffn_norm, final_norm, fox_w_in, fox_b_f, fox_qk_norm, fox_w_out,
           diff_w_in, diff_lambda, diff_subln, diff_w_out, rel_bias,
           ffn_w_in, ffn_conv_w, ffn_conv_b, ffn_w_out):
    h = x.reshape(TOKENS, D_MODEL)
    bias_tiles = _bias_tiles(rel_bias)
    for i in range(DEPTH):
        j = i // 2
        a = _rmsnorm(h, attn_norm[i], BF16)
        if i % 2 == 0:
            proj = _fox_proj(a, fox_w_in, j, fox_qk_norm[j])
            cq, ck = _fox_decay(a, fox_w_in[j, :, 4 * D_MODEL:], fox_b_f[j])
            o = _fox_attention(proj.reshape(BATCH, SEQ, 4 * D_MODEL),
                               cq.reshape(BATCH, SEQ, D_MODEL),
                               ck.reshape(BATCH, SEQ, D_MODEL))
            h = _residual_matmul(o.reshape(TOKENS, D_MODEL), fox_w_out, j, h, 512, 1024)
        else:
            lambda_init = 0.8 - 0.6 * math.exp(-0.3 * i)
            proj = _diff_proj(a, diff_w_in, j)
            o = _diff_attention(proj.reshape(BATCH, SEQ, 3 * D_MODEL), bias_tiles,
                                diff_lambda[j], diff_subln[j], lambda_init)
            h = _residual_matmul(o.reshape(TOKENS, D_MODEL), diff_w_out, j, h, 512, 1024)
        a = _rmsnorm(h, ffn_norm[i], BF16)
        y = _ffn_in(a, ffn_w_in, i, ffn_conv_w[i], ffn_conv_b[i])
        h = _residual_matmul(y, ffn_w_out, i, h, 512, 512)
    return _rmsnorm(h, final_norm, F32).reshape(BATCH, SEQ, D_MODEL)
```

```python
import functools
import math

import jax
import jax.numpy as jnp
import numpy as np
from jax import lax
from jax.experimental import pallas as pl
from jax.experimental.pallas import tpu as pltpu

D_MODEL = 2048
BATCH = 4
SEQ = 2048
TOKENS = BATCH * SEQ
DEPTH = 4
HEAD_DIM = 128
FOX_HEADS = D_MODEL // HEAD_DIM
DIFF_HEADS = D_MODEL // (2 * HEAD_DIM)
D_FF = 5632
CONV_WIDTH = 3
NUM_BUCKETS = 32
MAX_DISTANCE = 128
NORM_EPS = 1e-6
SUBLN_EPS = 1e-5

LANES = 128
VMEM_LIMIT = 56 * 1024 * 1024
NEG = -1e30
LOG2E = math.log2(math.e)
Q_SCALE = HEAD_DIM ** -0.5 * LOG2E

TQ = 256
TK = 256
NQ = SEQ // TQ
FOX_HPS = 2
FOX_SLOTS = 4
DIFF_SLOTS = 6

SUB_ROWS = 256
MM_TILES = {
    "fox_proj": (1024, 1024),
    "diff_proj": (1024, 1024),
    "attn_out": (1024, 1024),
    "ffn_in": (1024, 512),
    "ffn_out": (512, 512),
}

F32 = jnp.float32
BF16 = jnp.bfloat16


def _params(*sem):
    return pltpu.CompilerParams(dimension_semantics=sem, vmem_limit_bytes=VMEM_LIMIT)


def _rmsnorm_kernel(x_ref, g_ref, o_ref):
    x = x_ref[...]
    ms = jnp.mean(x * x, axis=-1, keepdims=True)
    o_ref[...] = (x * lax.rsqrt(ms + NORM_EPS) * g_ref[...]).astype(o_ref.dtype)


def _rmsnorm(h, g, out_dtype):
    tm = 512
    return pl.pallas_call(
        _rmsnorm_kernel,
        grid=(TOKENS // tm,),
        in_specs=[pl.BlockSpec((tm, D_MODEL), lambda m: (m, 0)),
                  pl.BlockSpec((1, D_MODEL), lambda m: (0, 0))],
        out_specs=pl.BlockSpec((tm, D_MODEL), lambda m: (m, 0)),
        out_shape=jax.ShapeDtypeStruct((TOKENS, D_MODEL), out_dtype),
        compiler_params=_params("arbitrary"),
        name="rmsnorm",
    )(h, g.reshape(1, D_MODEL))


def _head_rmsnorm(acc, gain, mult):
    ms = jnp.mean(acc * acc, axis=-1, keepdims=True)
    return acc * lax.rsqrt(ms + NORM_EPS) * (gain * mult)


def _row_blocks(tm):
    return range(0, tm, SUB_ROWS)


def _fox_proj_kernel(x_ref, wt_ref, qkn_ref, o_ref, wb_ref, *, tm, tn):
    n = pl.program_id(0)

    @pl.when(pl.program_id(1) == 0)
    def _():
        for c in range(0, tn, SUB_ROWS):
            wb_ref[:, c:c + SUB_ROWS] = wt_ref[c:c + SUB_ROWS, :].T.astype(BF16)

    per = D_MODEL // tn

    def body(epilogue):
        for r in _row_blocks(tm):
            rows = slice(r, r + SUB_ROWS)
            acc = jnp.dot(x_ref[rows, :], wb_ref[...], preferred_element_type=F32)
            epilogue(acc, rows)

    def normed(row, mult):
        gain = qkn_ref[row:row + 1, :]

        def epilogue(acc, rows):
            for c in range(tn // HEAD_DIM):
                sl = slice(c * HEAD_DIM, (c + 1) * HEAD_DIM)
                o_ref[rows, sl] = _head_rmsnorm(acc[:, sl], gain, mult).astype(BF16)
        return epilogue

    def plain(acc, rows):
        o_ref[rows, :] = acc.astype(BF16)

    def gated(acc, rows):
        o_ref[rows, :] = jax.nn.sigmoid(acc).astype(BF16)

    @pl.when(n < per)
    def _():
        body(normed(0, Q_SCALE))

    @pl.when((n >= per) & (n < 2 * per))
    def _():
        body(normed(1, 1.0))

    @pl.when((n >= 2 * per) & (n < 3 * per))
    def _():
        body(plain)

    @pl.when(n >= 3 * per)
    def _():
        body(gated)


def _fox_proj(a, w_in_t, layer, qk_norm):
    tm, tn = MM_TILES["fox_proj"]
    n_out = 4 * D_MODEL
    return pl.pallas_call(
        functools.partial(_fox_proj_kernel, tm=tm, tn=tn),
        grid=(n_out // tn, TOKENS // tm),
        in_specs=[pl.BlockSpec((tm, D_MODEL), lambda n, m: (m, 0)),
                  pl.BlockSpec((None, tn, D_MODEL), lambda n, m: (layer, n, 0)),
                  pl.BlockSpec((2, HEAD_DIM), lambda n, m: (0, 0))],
        out_specs=pl.BlockSpec((tm, tn), lambda n, m: (m, n)),
        out_shape=jax.ShapeDtypeStruct((TOKENS, n_out), BF16),
        scratch_shapes=[pltpu.VMEM((D_MODEL, tn), BF16)],
        compiler_params=_params("arbitrary", "arbitrary"),
        name="fox_proj",
    )(a, w_in_t, qk_norm)


def _diff_proj_kernel(x_ref, w_ref, o_ref, wb_ref, *, tm, tn):
    n = pl.program_id(0)

    @pl.when(pl.program_id(1) == 0)
    def _():
        wb_ref[...] = w_ref[...].astype(BF16)

    mult = jnp.where(n < D_MODEL // tn, Q_SCALE, 1.0).astype(F32)
    for r in _row_blocks(tm):
        rows = slice(r, r + SUB_ROWS)
        acc = jnp.dot(x_ref[rows, :], wb_ref[...], preferred_element_type=F32)
        o_ref[rows, :] = (acc * mult).astype(BF16)


def _diff_proj(a, w_in, layer):
    tm, tn = MM_TILES["diff_proj"]
    n_out = 3 * D_MODEL
    return pl.pallas_call(
        functools.partial(_diff_proj_kernel, tm=tm, tn=tn),
        grid=(n_out // tn, TOKENS // tm),
        in_specs=[pl.BlockSpec((tm, D_MODEL), lambda n, m: (m, 0)),
                  pl.BlockSpec((None, D_MODEL, tn), lambda n, m: (layer, 0, n))],
        out_specs=pl.BlockSpec((tm, tn), lambda n, m: (m, n)),
        out_shape=jax.ShapeDtypeStruct((TOKENS, n_out), BF16),
        scratch_shapes=[pltpu.VMEM((D_MODEL, tn), BF16)],
        compiler_params=_params("arbitrary", "arbitrary"),
        name="diff_proj",
    )(a, w_in)


def _residual_matmul_kernel(x_ref, w_ref, h_ref, o_ref, wb_ref, *, tm):
    @pl.when(pl.program_id(1) == 0)
    def _():
        wb_ref[...] = w_ref[...].astype(BF16)

    for r in _row_blocks(tm):
        rows = slice(r, r + SUB_ROWS)
        acc = jnp.dot(x_ref[rows, :], wb_ref[...], preferred_element_type=F32)
        o_ref[rows, :] = h_ref[rows, :] + acc


def _residual_matmul(x, w, layer, h, tiles):
    tm, tn = MM_TILES[tiles]
    k = x.shape[1]
    return pl.pallas_call(
        functools.partial(_residual_matmul_kernel, tm=tm),
        grid=(D_MODEL // tn, TOKENS // tm),
        in_specs=[pl.BlockSpec((tm, k), lambda n, m: (m, 0)),
                  pl.BlockSpec((None, k, tn), lambda n, m: (layer, 0, n)),
                  pl.BlockSpec((tm, tn), lambda n, m: (m, n))],
        out_specs=pl.BlockSpec((tm, tn), lambda n, m: (m, n)),
        out_shape=jax.ShapeDtypeStruct((TOKENS, D_MODEL), F32),
        scratch_shapes=[pltpu.VMEM((k, tn), BF16)],
        compiler_params=_params("arbitrary", "arbitrary"),
        name="residual_matmul",
    )(x, w, h)


CARRY_ROWS = 8


def _ffn_in_kernel(x_ref, wg_ref, wu_ref, cw_ref, cb_ref, o_ref,
                   wgb_ref, wub_ref, gs_ref, *, tm):
    m = pl.program_id(1)

    @pl.when(m == 0)
    def _():
        wgb_ref[...] = wg_ref[...].astype(BF16)
        wub_ref[...] = wu_ref[...].astype(BF16)

    @pl.when(m % (SEQ // tm) == 0)
    def _():
        gs_ref[0:CARRY_ROWS, :] = jnp.zeros((CARRY_ROWS, gs_ref.shape[1]), F32)

    for r in _row_blocks(tm):
        x = x_ref[r:r + SUB_ROWS, :]
        g = jnp.dot(x, wgb_ref[...], preferred_element_type=F32)
        u = jnp.dot(x, wub_ref[...], preferred_element_type=F32)
        base = CARRY_ROWS + r
        gs_ref[base:base + SUB_ROWS, :] = g
        g1 = gs_ref[base - 1:base - 1 + SUB_ROWS, :]
        g2 = gs_ref[base - 2:base - 2 + SUB_ROWS, :]
        conv = (cw_ref[2:3, :] * g + cw_ref[1:2, :] * g1 + cw_ref[0:1, :] * g2
                + cb_ref[...])
        gelu = 0.5 * conv * (1.0 + lax.erf(conv * (2.0 ** -0.5)))
        o_ref[r:r + SUB_ROWS, :] = (gelu * u).astype(BF16)
    gs_ref[0:CARRY_ROWS, :] = gs_ref[tm:tm + CARRY_ROWS, :]


def _ffn_in(a, w_in, layer, conv_w, conv_b):
    tm, tn = MM_TILES["ffn_in"]
    nt = D_FF // tn
    return pl.pallas_call(
        functools.partial(_ffn_in_kernel, tm=tm),
        grid=(nt, TOKENS // tm),
        in_specs=[pl.BlockSpec((tm, D_MODEL), lambda n, m: (m, 0)),
                  pl.BlockSpec((None, D_MODEL, tn), lambda n, m: (layer, 0, n)),
                  pl.BlockSpec((None, D_MODEL, tn), lambda n, m: (layer, 0, nt + n)),
                  pl.BlockSpec((CONV_WIDTH, tn), lambda n, m: (0, n)),
                  pl.BlockSpec((1, tn), lambda n, m: (0, n))],
        out_specs=pl.BlockSpec((tm, tn), lambda n, m: (m, n)),
        out_shape=jax.ShapeDtypeStruct((TOKENS, D_FF), BF16),
        scratch_shapes=[pltpu.VMEM((D_MODEL, tn), BF16),
                        pltpu.VMEM((D_MODEL, tn), BF16),
                        pltpu.VMEM((tm + CARRY_ROWS, tn), F32)],
        compiler_params=_params("arbitrary", "arbitrary"),
        name="ffn_in",
    )(a, w_in, w_in, conv_w, conv_b.reshape(1, D_FF))


def _split3(x):
    hi = x.astype(BF16)
    r1 = x - hi.astype(F32)
    mid = r1.astype(BF16)
    lo = (r1 - mid.astype(F32)).astype(BF16)
    return jnp.concatenate([hi, mid, lo], axis=-1)


CUM_BLOCK = 512


def _fox_decay_kernel(x_ref, wt_ref, b_ref, selq_ref, selk_ref, oneq_ref, onek_ref,
                      cq_ref, ck_ref, c_ref):
    wt = jnp.concatenate([wt_ref[...], jnp.zeros((LANES - FOX_HEADS, D_MODEL), F32)],
                         axis=0).astype(BF16)
    fl = lax.dot_general(x_ref[...], wt, _NT, preferred_element_type=F32)
    z = fl + b_ref[...]
    logf = (jnp.minimum(z, 0.0) - jnp.log1p(jnp.exp(-jnp.abs(z)))) * LOG2E
    row = lax.broadcasted_iota(jnp.int32, (CUM_BLOCK, CUM_BLOCK), 0)
    col = lax.broadcasted_iota(jnp.int32, (CUM_BLOCK, CUM_BLOCK), 1)
    tri = jnp.where(row >= col, 1.0, 0.0).astype(BF16)
    carry = jnp.zeros((1, LANES), F32)
    for blk in range(SEQ // CUM_BLOCK):
        sl = slice(blk * CUM_BLOCK, (blk + 1) * CUM_BLOCK)
        cs = jnp.dot(tri, _split3(logf[sl]), preferred_element_type=F32)
        c_blk = (cs[:, :LANES] + cs[:, LANES:2 * LANES] + cs[:, 2 * LANES:]) + carry
        carry = c_blk[CUM_BLOCK - 1:CUM_BLOCK, :]
        c_ref[sl, :] = c_blk
    parts = _split3(c_ref[...])
    cq = jnp.dot(parts, selq_ref[...], preferred_element_type=F32) + oneq_ref[...]
    ck = jnp.dot(parts, selk_ref[...], preferred_element_type=F32) + onek_ref[...]
    cq_ref[...] = cq.astype(BF16)
    ck_ref[...] = ck.astype(BF16)


def _decay_selectors():
    selq = np.zeros((3 * LANES, D_MODEL), np.float32)
    selk = np.zeros((3 * LANES, D_MODEL), np.float32)
    oneq = np.zeros((1, D_MODEL), np.float32)
    onek = np.zeros((1, D_MODEL), np.float32)
    for h in range(FOX_HEADS):
        for p in range(3):
            selq[p * LANES + h, h * HEAD_DIM + p] = 1.0
            selk[p * LANES + h, h * HEAD_DIM + 3 + p] = -1.0
            oneq[0, h * HEAD_DIM + 3 + p] = 1.0
            onek[0, h * HEAD_DIM + p] = 1.0
    return (jnp.asarray(selq, BF16), jnp.asarray(selk, BF16),
            jnp.asarray(oneq), jnp.asarray(onek))


def _fox_decay(a, w_in_t, layer, b_f):
    b_pad = jnp.pad(b_f, (0, LANES - FOX_HEADS)).reshape(1, LANES)
    selq, selk, oneq, onek = _decay_selectors()
    full = lambda shape: pl.BlockSpec(shape, lambda b: (0, 0))
    out = jax.ShapeDtypeStruct((TOKENS, D_MODEL), BF16)
    gate_block = 4 * D_MODEL // FOX_HEADS
    return pl.pallas_call(
        _fox_decay_kernel,
        grid=(BATCH,),
        in_specs=[pl.BlockSpec((SEQ, D_MODEL), lambda b: (b, 0)),
                  pl.BlockSpec((None, FOX_HEADS, D_MODEL), lambda b: (layer, gate_block, 0)),
                  full((1, LANES)),
                  full((3 * LANES, D_MODEL)), full((3 * LANES, D_MODEL)),
                  full((1, D_MODEL)), full((1, D_MODEL))],
        out_specs=[pl.BlockSpec((SEQ, D_MODEL), lambda b: (b, 0)),
                   pl.BlockSpec((SEQ, D_MODEL), lambda b: (b, 0))],
        out_shape=[out, out],
        scratch_shapes=[pltpu.VMEM((SEQ, LANES), F32)],
        compiler_params=_params("arbitrary"),
        name="fox_decay",
    )(a, w_in_t, b_pad, selq, selk, oneq, onek)


_NT = (((1,), (1,)), ((), ()))


def _transpose_values(v_ref, lanes, vt_ref):
    for c in range(0, SEQ, 512):
        vt_ref[:, c:c + 512] = v_ref[0, c:c + 512, lanes].astype(F32).T.astype(BF16)


def _softmax_columns(sr, pr, n_tiles):
    m = jnp.max(sr[:n_tiles * TK, :], axis=0, keepdims=True)
    l = jnp.zeros((1, TQ), F32)
    for j in range(n_tiles):
        rows = slice(j * TK, (j + 1) * TK)
        p = jnp.exp2(sr[rows, :] - m)
        l = l + jnp.sum(p, axis=0, keepdims=True)
        pr[rows, :] = p.astype(BF16)
    return l


def _staged(n_units, stage_scores, stage_softmax, stage_output):
    for t in range(n_units + 2):
        if t >= 2:
            stage_output(t - 2)
        if 1 <= t <= n_units:
            stage_softmax(t - 1)
        if t < n_units:
            stage_scores(t)


def _fox_attn_kernel(q_ref, cq_ref, k_ref, ck_ref, v_ref, g_ref, o_ref,
                     s_ref, p_ref, vt_ref):
    krow = lax.broadcasted_iota(jnp.int32, (TK, TQ), 0)
    qcol = lax.broadcasted_iota(jnp.int32, (TK, TQ), 1)
    causal = krow <= qcol
    head_lanes = [slice(hh * HEAD_DIM, (hh + 1) * HEAD_DIM) for hh in range(FOX_HPS)]
    for hh in range(FOX_HPS):
        _transpose_values(v_ref, head_lanes[hh], vt_ref.at[hh])
    units = [(hh, i) for i in range(NQ) for hh in range(FOX_HPS)]
    sums = {}

    def stage_scores(u):
        hh, i = units[u]
        hl = head_lanes[hh]
        qs, keys = i * TQ, i * TQ + TK
        sr = s_ref.at[u % FOX_SLOTS]
        qa = jnp.concatenate([q_ref[0, qs:qs + TQ, hl], cq_ref[0, qs:qs + TQ, hl]], axis=-1)
        ka = jnp.concatenate([k_ref[0, :keys, hl], ck_ref[0, :keys, hl]], axis=-1)
        s = lax.dot_general(ka, qa, _NT, preferred_element_type=F32)
        if i > 0:
            sr[:qs, :] = s[:qs]
        sr[qs:keys, :] = jnp.where(causal, s[qs:keys], NEG)

    def stage_softmax(u):
        _, i = units[u]
        sums[u] = _softmax_columns(s_ref.at[u % FOX_SLOTS], p_ref.at[u % FOX_SLOTS], i + 1)

    def stage_output(u):
        hh, i = units[u]
        hl = head_lanes[hh]
        qs, keys = i * TQ, i * TQ + TK
        pr = p_ref.at[u % FOX_SLOTS]
        acc_t = jnp.dot(vt_ref[hh, :, :keys], pr[:keys, :], preferred_element_type=F32)
        out = (acc_t * (1.0 / sums.pop(u))).T
        gate = g_ref[0, qs:qs + TQ, hl].astype(F32)
        o_ref[0, qs:qs + TQ, hl] = (out * gate).astype(BF16)

    _staged(len(units), stage_scores, stage_softmax, stage_output)


def _fox_attention(proj, cq, ck):
    width = FOX_HPS * HEAD_DIM
    groups = FOX_HEADS // FOX_HPS
    blk = lambda off: pl.BlockSpec((1, SEQ, width), lambda b, h: (b, 0, off * groups + h))
    return pl.pallas_call(
        _fox_attn_kernel,
        grid=(BATCH, groups),
        in_specs=[blk(0), blk(0), blk(1), blk(0), blk(2), blk(3)],
        out_specs=blk(0),
        out_shape=jax.ShapeDtypeStruct((BATCH, SEQ, D_MODEL), BF16),
        scratch_shapes=[pltpu.VMEM((FOX_SLOTS, SEQ, TQ), F32),
                        pltpu.VMEM((FOX_SLOTS, SEQ, TQ), BF16),
                        pltpu.VMEM((FOX_HPS, HEAD_DIM, SEQ), BF16)],
        compiler_params=_params("arbitrary", "arbitrary"),
        name="fox_attn",
    )(proj, cq, proj, ck, proj, proj)


def _bias_tile_kernel(rb_ref, o_ref):
    mp = pl.program_id(0)
    krow = lax.broadcasted_iota(jnp.int32, (TK, TQ), 0)
    qcol = lax.broadcasted_iota(jnp.int32, (TK, TQ), 1)
    max_exact = NUM_BUCKETS // 2
    for t in range(2):
        n = jnp.maximum(qcol - krow + t * TK, 0)
        nf = jnp.maximum(n, 1).astype(F32)
        large = max_exact + (jnp.log(nf / max_exact) / math.log(MAX_DISTANCE / max_exact)
                             * (NUM_BUCKETS - max_exact)).astype(jnp.int32)
        large = jnp.minimum(large, NUM_BUCKETS - 1)
        bucket = jnp.where(n < max_exact, n, large)
        val = jnp.zeros((TK, TQ), F32)
        for bkt in range(NUM_BUCKETS):
            val = jnp.where(bucket == bkt, rb_ref[bkt, mp], val)
        val = (val - rb_ref[NUM_BUCKETS - 1, mp]) * LOG2E
        if t == 0:
            val = jnp.where(krow <= qcol, val, NEG)
        o_ref[0, t] = val


def _bias_tiles(rel_bias):
    n_maps = 2 * DIFF_HEADS
    return pl.pallas_call(
        _bias_tile_kernel,
        grid=(n_maps,),
        in_specs=[pl.BlockSpec(memory_space=pltpu.SMEM)],
        out_specs=pl.BlockSpec((1, 2, TK, TQ), lambda mp: (mp, 0, 0, 0)),
        out_shape=jax.ShapeDtypeStruct((n_maps, 2, TK, TQ), F32),
        compiler_params=_params("arbitrary"),
        name="bias_tiles",
    )(rel_bias.reshape(NUM_BUCKETS, n_maps))


def _diff_attn_kernel(q1_ref, q2_ref, k1_ref, k2_ref, v_ref, bias_ref, lam_ref,
                      sub_ref, o_ref, s_ref, p_ref, vt_ref, *, lambda_init):
    lamf = lam_ref[...]
    lam_full = (jnp.exp(jnp.sum(lamf[0:1] * lamf[1:2], axis=-1, keepdims=True))
                - jnp.exp(jnp.sum(lamf[2:3] * lamf[3:4], axis=-1, keepdims=True))
                + lambda_init)
    _transpose_values(v_ref, slice(None), vt_ref)
    maps = ((q1_ref, k1_ref), (q2_ref, k2_ref))
    sums = {}

    def slot(i, mp):
        return 2 * (i % (DIFF_SLOTS // 2)) + mp

    def stage_scores(i):
        qs, keys = i * TQ, i * TQ + TK
        near = qs - TK
        for mp, (q_ref, k_ref) in enumerate(maps):
            sr = s_ref.at[slot(i, mp)]
            s = lax.dot_general(k_ref[0, :keys, :], q_ref[0, qs:qs + TQ, :], _NT,
                                preferred_element_type=F32)
            if i > 1:
                sr[:near, :] = s[:near]
            if i > 0:
                sr[near:qs, :] = s[near:qs] + bias_ref[mp, 1]
            sr[qs:keys, :] = s[qs:keys] + bias_ref[mp, 0]

    def stage_softmax(i):
        for mp in range(2):
            sl = slot(i, mp)
            sums[(i, mp)] = _softmax_columns(s_ref.at[sl], p_ref.at[sl], i + 1)

    def stage_output(i):
        qs, keys = i * TQ, i * TQ + TK
        outs = []
        for mp in range(2):
            pr = p_ref.at[slot(i, mp)]
            acc_t = jnp.dot(vt_ref[:, :keys], pr[:keys, :], preferred_element_type=F32)
            outs.append(acc_t * (1.0 / sums.pop((i, mp))))
        o = (outs[0] - lam_full * outs[1]).T
        ms = jnp.mean(o * o, axis=-1, keepdims=True)
        o = o * lax.rsqrt(ms + SUBLN_EPS) * sub_ref[...] * (1.0 - lambda_init)
        o_ref[0, qs:qs + TQ, :] = o.astype(BF16)

    _staged(NQ, stage_scores, stage_softmax, stage_output)


def _diff_attention(proj, bias_tiles, lam, subln, lambda_init):
    qk = lambda off, j: pl.BlockSpec((1, SEQ, HEAD_DIM),
                                     lambda h, b: (b, 0, off + 2 * h + j))
    n_maps = 2 * DIFF_HEADS
    width = 2 * HEAD_DIM
    return pl.pallas_call(
        functools.partial(_diff_attn_kernel, lambda_init=lambda_init),
        grid=(DIFF_HEADS, BATCH),
        in_specs=[qk(0, 0), qk(0, 1), qk(n_maps, 0), qk(n_maps, 1),
                  pl.BlockSpec((1, SEQ, width), lambda h, b: (b, 0, 2 * DIFF_HEADS + h)),
                  pl.BlockSpec((2, 2, TK, TQ), lambda h, b: (h, 0, 0, 0)),
                  pl.BlockSpec((4, HEAD_DIM), lambda h, b: (0, 0)),
                  pl.BlockSpec((1, width), lambda h, b: (0, 0))],
        out_specs=pl.BlockSpec((1, SEQ, width), lambda h, b: (b, 0, h)),
        out_shape=jax.ShapeDtypeStruct((BATCH, SEQ, D_MODEL), BF16),
        scratch_shapes=[pltpu.VMEM((DIFF_SLOTS, SEQ, TQ), F32),
                        pltpu.VMEM((DIFF_SLOTS, SEQ, TQ), BF16),
                        pltpu.VMEM((width, SEQ), BF16)],
        compiler_params=_params("arbitrary", "arbitrary"),
        name="diff_attn",
    )(proj, proj, proj, proj, proj, bias_tiles, lam, subln.reshape(1, width))


def kernel(x, attn_norm, ffn_norm, final_norm, fox_w_in, fox_b_f, fox_qk_norm, fox_w_out,
           diff_w_in, diff_lambda, diff_subln, diff_w_out, rel_bias,
           ffn_w_in, ffn_conv_w, ffn_conv_b, ffn_w_out):
    h = x.reshape(TOKENS, D_MODEL)
    bias_tiles = _bias_tiles(rel_bias)
    fox_w_in_t = jnp.swapaxes(fox_w_in, 1, 2)
    for i in range(DEPTH):
        j = i // 2
        a = _rmsnorm(h, attn_norm[i], BF16)
        if i % 2 == 0:
            proj = _fox_proj(a, fox_w_in_t, j, fox_qk_norm[j])
            cq, ck = _fox_decay(a, fox_w_in_t, j, fox_b_f[j])
            o = _fox_attention(proj.reshape(BATCH, SEQ, 4 * D_MODEL),
                               cq.reshape(BATCH, SEQ, D_MODEL),
                               ck.reshape(BATCH, SEQ, D_MODEL))
            h = _residual_matmul(o.reshape(TOKENS, D_MODEL), fox_w_out, j, h, "attn_out")
        else:
            lambda_init = 0.8 - 0.6 * math.exp(-0.3 * i)
            proj = _diff_proj(a, diff_w_in, j)
            o = _diff_attention(proj.reshape(BATCH, SEQ, 3 * D_MODEL), bias_tiles,
                                diff_lambda[j], diff_subln[j], lambda_init)
            h = _residual_matmul(o.reshape(TOKENS, D_MODEL), diff_w_out, j, h, "attn_out")
        a = _rmsnorm(h, ffn_norm[i], BF16)
        y = _ffn_in(a, ffn_w_in, i, ffn_conv_w[i], ffn_conv_b[i])
        h = _residual_matmul(y, ffn_w_out, i, h, "ffn_out")
    return _rmsnorm(h, final_norm, F32).reshape(BATCH, SEQ, D_MODEL)
```

```python
import functools
import math

import jax
import jax.numpy as jnp
import numpy as np
from jax import lax
from jax.experimental import pallas as pl
from jax.experimental.pallas import tpu as pltpu

D_MODEL = 2048
BATCH = 4
SEQ = 2048
TOKENS = BATCH * SEQ
DEPTH = 4
HEAD_DIM = 128
FOX_HEADS = D_MODEL // HEAD_DIM
DIFF_HEADS = D_MODEL // (2 * HEAD_DIM)
D_FF = 5632
CONV_WIDTH = 3
NUM_BUCKETS = 32
MAX_DISTANCE = 128
NORM_EPS = 1e-6
SUBLN_EPS = 1e-5

LANES = 128
VMEM_LIMIT = 56 * 1024 * 1024
NEG = -1e30
LOG2E = math.log2(math.e)
Q_SCALE = HEAD_DIM ** -0.5 * LOG2E

TQ = 256
TK = 256
NQ = SEQ // TQ
FOX_HPS = 2
FOX_SLOTS = 4
DIFF_SLOTS = 6

SUB_ROWS = 256
MM_TILES = {
    "fox_proj": (2048, 1024),
    "diff_proj": (2048, 1024),
    "attn_out": (512, D_MODEL),
    "ffn_in": (2048, 512),
    "ffn_out": (512, 512),
}

F32 = jnp.float32
BF16 = jnp.bfloat16


def _params(*sem):
    return pltpu.CompilerParams(dimension_semantics=sem, vmem_limit_bytes=VMEM_LIMIT)


def _rmsnorm_kernel(x_ref, g_ref, o_ref):
    x = x_ref[...]
    ms = jnp.mean(x * x, axis=-1, keepdims=True)
    o_ref[...] = (x * lax.rsqrt(ms + NORM_EPS) * g_ref[...]).astype(o_ref.dtype)


def _rmsnorm(h, g, out_dtype):
    tm = 512
    return pl.pallas_call(
        _rmsnorm_kernel,
        grid=(TOKENS // tm,),
        in_specs=[pl.BlockSpec((tm, D_MODEL), lambda m: (m, 0)),
                  pl.BlockSpec((1, D_MODEL), lambda m: (0, 0))],
        out_specs=pl.BlockSpec((tm, D_MODEL), lambda m: (m, 0)),
        out_shape=jax.ShapeDtypeStruct((TOKENS, D_MODEL), out_dtype),
        compiler_params=_params("arbitrary"),
        name="rmsnorm",
    )(h, g.reshape(1, D_MODEL))


def _head_rmsnorm(acc, gain, mult):
    ms = jnp.mean(acc * acc, axis=-1, keepdims=True)
    return acc * lax.rsqrt(ms + NORM_EPS) * (gain * mult)


def _row_blocks(tm):
    return range(0, tm, SUB_ROWS)


def _fox_proj_kernel(x_ref, wt_ref, qkn_ref, o_ref, wb_ref, *, tm, tn):
    n = pl.program_id(0)

    @pl.when(pl.program_id(1) == 0)
    def _():
        for c in range(0, tn, SUB_ROWS):
            wb_ref[:, c:c + SUB_ROWS] = wt_ref[c:c + SUB_ROWS, :].T.astype(BF16)

    per = D_MODEL // tn

    def body(epilogue):
        for r in _row_blocks(tm):
            rows = slice(r, r + SUB_ROWS)
            acc = jnp.dot(x_ref[rows, :], wb_ref[...], preferred_element_type=F32)
            epilogue(acc, rows)

    def normed(row, mult):
        gain = qkn_ref[row:row + 1, :]

        def epilogue(acc, rows):
            for c in range(tn // HEAD_DIM):
                sl = slice(c * HEAD_DIM, (c + 1) * HEAD_DIM)
                o_ref[rows, sl] = _head_rmsnorm(acc[:, sl], gain, mult).astype(BF16)
        return epilogue

    def plain(acc, rows):
        o_ref[rows, :] = acc.astype(BF16)

    def gated(acc, rows):
        o_ref[rows, :] = jax.nn.sigmoid(acc).astype(BF16)

    @pl.when(n < per)
    def _():
        body(normed(0, Q_SCALE))

    @pl.when((n >= per) & (n < 2 * per))
    def _():
        body(normed(1, 1.0))

    @pl.when((n >= 2 * per) & (n < 3 * per))
    def _():
        body(plain)

    @pl.when(n >= 3 * per)
    def _():
        body(gated)


def _fox_proj(a, w_in_t, layer, qk_norm):
    tm, tn = MM_TILES["fox_proj"]
    n_out = 4 * D_MODEL
    return pl.pallas_call(
        functools.partial(_fox_proj_kernel, tm=tm, tn=tn),
        grid=(n_out // tn, TOKENS // tm),
        in_specs=[pl.BlockSpec((tm, D_MODEL), lambda n, m: (m, 0)),
                  pl.BlockSpec((None, tn, D_MODEL), lambda n, m: (layer, n, 0)),
                  pl.BlockSpec((2, HEAD_DIM), lambda n, m: (0, 0))],
        out_specs=pl.BlockSpec((tm, tn), lambda n, m: (m, n)),
        out_shape=jax.ShapeDtypeStruct((TOKENS, n_out), BF16),
        scratch_shapes=[pltpu.VMEM((D_MODEL, tn), BF16)],
        compiler_params=_params("arbitrary", "arbitrary"),
        name="fox_proj",
    )(a, w_in_t, qk_norm)


def _diff_proj_kernel(x_ref, w_ref, o_ref, wb_ref, *, tm, tn):
    n = pl.program_id(0)

    @pl.when(pl.program_id(1) == 0)
    def _():
        wb_ref[...] = w_ref[...].astype(BF16)

    mult = jnp.where(n < D_MODEL // tn, Q_SCALE, 1.0).astype(F32)
    for r in _row_blocks(tm):
        rows = slice(r, r + SUB_ROWS)
        acc = jnp.dot(x_ref[rows, :], wb_ref[...], preferred_element_type=F32)
        o_ref[rows, :] = (acc * mult).astype(BF16)


def _diff_proj(a, w_in, layer):
    tm, tn = MM_TILES["diff_proj"]
    n_out = 3 * D_MODEL
    return pl.pallas_call(
        functools.partial(_diff_proj_kernel, tm=tm, tn=tn),
        grid=(n_out // tn, TOKENS // tm),
        in_specs=[pl.BlockSpec((tm, D_MODEL), lambda n, m: (m, 0)),
                  pl.BlockSpec((None, D_MODEL, tn), lambda n, m: (layer, 0, n))],
        out_specs=pl.BlockSpec((tm, tn), lambda n, m: (m, n)),
        out_shape=jax.ShapeDtypeStruct((TOKENS, n_out), BF16),
        scratch_shapes=[pltpu.VMEM((D_MODEL, tn), BF16)],
        compiler_params=_params("arbitrary", "arbitrary"),
        name="diff_proj",
    )(a, w_in)


def _residual_matmul_kernel(x_ref, w_ref, h_ref, o_ref, wb_ref, *, tm):
    @pl.when(pl.program_id(1) == 0)
    def _():
        wb_ref[...] = w_ref[...].astype(BF16)

    for r in _row_blocks(tm):
        rows = slice(r, r + SUB_ROWS)
        acc = jnp.dot(x_ref[rows, :], wb_ref[...], preferred_element_type=F32)
        o_ref[rows, :] = h_ref[rows, :] + acc


def _residual_matmul(x, w, layer, h, tiles):
    tm, tn = MM_TILES[tiles]
    k = x.shape[1]
    return pl.pallas_call(
        functools.partial(_residual_matmul_kernel, tm=tm),
        grid=(D_MODEL // tn, TOKENS // tm),
        in_specs=[pl.BlockSpec((tm, k), lambda n, m: (m, 0)),
                  pl.BlockSpec((None, k, tn), lambda n, m: (layer, 0, n)),
                  pl.BlockSpec((tm, tn), lambda n, m: (m, n))],
        out_specs=pl.BlockSpec((tm, tn), lambda n, m: (m, n)),
        out_shape=jax.ShapeDtypeStruct((TOKENS, D_MODEL), F32),
        scratch_shapes=[pltpu.VMEM((k, tn), BF16)],
        compiler_params=_params("arbitrary", "arbitrary"),
        name="residual_matmul",
    )(x, w, h)


def _attn_out_kernel(x_ref, w_ref, h_ref, g_ref, o_ref, a_ref, wb_ref, *, tm):
    @pl.when(pl.program_id(0) == 0)
    def _():
        for c in range(0, D_MODEL, 512):
            wb_ref[c:c + 512, :] = w_ref[c:c + 512, :].astype(BF16)

    for r in _row_blocks(tm):
        rows = slice(r, r + SUB_ROWS)
        acc = jnp.dot(x_ref[rows, :], wb_ref[...], preferred_element_type=F32)
        hn = h_ref[rows, :] + acc
        o_ref[rows, :] = hn
        ms = jnp.mean(hn * hn, axis=-1, keepdims=True)
        a_ref[rows, :] = (hn * lax.rsqrt(ms + NORM_EPS) * g_ref[...]).astype(BF16)


def _attn_out(x, w, layer, h, g):
    tm, tn = MM_TILES["attn_out"]
    assert tn == D_MODEL
    rows = lambda m: (m, 0)
    return pl.pallas_call(
        functools.partial(_attn_out_kernel, tm=tm),
        grid=(TOKENS // tm,),
        in_specs=[pl.BlockSpec((tm, D_MODEL), rows),
                  pl.BlockSpec((None, D_MODEL, D_MODEL), lambda m: (layer, 0, 0),
                               pipeline_mode=pl.Buffered(1)),
                  pl.BlockSpec((tm, D_MODEL), rows),
                  pl.BlockSpec((1, D_MODEL), lambda m: (0, 0))],
        out_specs=[pl.BlockSpec((tm, D_MODEL), rows), pl.BlockSpec((tm, D_MODEL), rows)],
        out_shape=[jax.ShapeDtypeStruct((TOKENS, D_MODEL), F32),
                   jax.ShapeDtypeStruct((TOKENS, D_MODEL), BF16)],
        scratch_shapes=[pltpu.VMEM((D_MODEL, D_MODEL), BF16)],
        compiler_params=_params("arbitrary"),
        name="attn_out",
    )(x, w, h, g.reshape(1, D_MODEL))


CARRY_ROWS = 8


def _ffn_in_kernel(x_ref, wg_ref, wu_ref, cw_ref, cb_ref, o_ref,
                   wgb_ref, wub_ref, gs_ref, *, tm):
    m = pl.program_id(1)

    @pl.when(m == 0)
    def _():
        wgb_ref[...] = wg_ref[...].astype(BF16)
        wub_ref[...] = wu_ref[...].astype(BF16)

    @pl.when(m % (SEQ // tm) == 0)
    def _():
        gs_ref[0:CARRY_ROWS, :] = jnp.zeros((CARRY_ROWS, gs_ref.shape[1]), F32)

    for r in _row_blocks(tm):
        x = x_ref[r:r + SUB_ROWS, :]
        g = jnp.dot(x, wgb_ref[...], preferred_element_type=F32)
        u = jnp.dot(x, wub_ref[...], preferred_element_type=F32)
        base = CARRY_ROWS + r
        gs_ref[base:base + SUB_ROWS, :] = g
        g1 = gs_ref[base - 1:base - 1 + SUB_ROWS, :]
        g2 = gs_ref[base - 2:base - 2 + SUB_ROWS, :]
        conv = (cw_ref[2:3, :] * g + cw_ref[1:2, :] * g1 + cw_ref[0:1, :] * g2
                + cb_ref[...])
        gelu = 0.5 * conv * (1.0 + lax.erf(conv * (2.0 ** -0.5)))
        o_ref[r:r + SUB_ROWS, :] = (gelu * u).astype(BF16)
    gs_ref[0:CARRY_ROWS, :] = gs_ref[tm:tm + CARRY_ROWS, :]


def _ffn_in(a, w_in, layer, conv_w, conv_b):
    tm, tn = MM_TILES["ffn_in"]
    nt = D_FF // tn
    return pl.pallas_call(
        functools.partial(_ffn_in_kernel, tm=tm),
        grid=(nt, TOKENS // tm),
        in_specs=[pl.BlockSpec((tm, D_MODEL), lambda n, m: (m, 0)),
                  pl.BlockSpec((None, D_MODEL, tn), lambda n, m: (layer, 0, n)),
                  pl.BlockSpec((None, D_MODEL, tn), lambda n, m: (layer, 0, nt + n)),
                  pl.BlockSpec((CONV_WIDTH, tn), lambda n, m: (0, n)),
                  pl.BlockSpec((1, tn), lambda n, m: (0, n))],
        out_specs=pl.BlockSpec((tm, tn), lambda n, m: (m, n)),
        out_shape=jax.ShapeDtypeStruct((TOKENS, D_FF), BF16),
        scratch_shapes=[pltpu.VMEM((D_MODEL, tn), BF16),
                        pltpu.VMEM((D_MODEL, tn), BF16),
                        pltpu.VMEM((tm + CARRY_ROWS, tn), F32)],
        compiler_params=_params("arbitrary", "arbitrary"),
        name="ffn_in",
    )(a, w_in, w_in, conv_w, conv_b.reshape(1, D_FF))


def _split3(x):
    hi = x.astype(BF16)
    r1 = x - hi.astype(F32)
    mid = r1.astype(BF16)
    lo = (r1 - mid.astype(F32)).astype(BF16)
    return jnp.concatenate([hi, mid, lo], axis=-1)


CUM_BLOCK = 512


def _fox_decay_kernel(x_ref, wt_ref, b_ref, selq_ref, selk_ref, oneq_ref, onek_ref,
                      cq_ref, ck_ref, c_ref):
    wt = jnp.concatenate([wt_ref[...], jnp.zeros((LANES - FOX_HEADS, D_MODEL), F32)],
                         axis=0).astype(BF16)
    fl = lax.dot_general(x_ref[...], wt, _NT, preferred_element_type=F32)
    z = fl + b_ref[...]
    logf = (jnp.minimum(z, 0.0) - jnp.log1p(jnp.exp(-jnp.abs(z)))) * LOG2E
    row = lax.broadcasted_iota(jnp.int32, (CUM_BLOCK, CUM_BLOCK), 0)
    col = lax.broadcasted_iota(jnp.int32, (CUM_BLOCK, CUM_BLOCK), 1)
    tri = jnp.where(row >= col, 1.0, 0.0).astype(BF16)
    carry = jnp.zeros((1, LANES), F32)
    for blk in range(SEQ // CUM_BLOCK):
        sl = slice(blk * CUM_BLOCK, (blk + 1) * CUM_BLOCK)
        cs = jnp.dot(tri, _split3(logf[sl]), preferred_element_type=F32)
        c_blk = (cs[:, :LANES] + cs[:, LANES:2 * LANES] + cs[:, 2 * LANES:]) + carry
        carry = c_blk[CUM_BLOCK - 1:CUM_BLOCK, :]
        c_ref[sl, :] = c_blk
    parts = _split3(c_ref[...])
    cq = jnp.dot(parts, selq_ref[...], preferred_element_type=F32) + oneq_ref[...]
    ck = jnp.dot(parts, selk_ref[...], preferred_element_type=F32) + onek_ref[...]
    cq_ref[...] = cq.astype(BF16)
    ck_ref[...] = ck.astype(BF16)


def _decay_selectors():
    selq = np.zeros((3 * LANES, D_MODEL), np.float32)
    selk = np.zeros((3 * LANES, D_MODEL), np.float32)
    oneq = np.zeros((1, D_MODEL), np.float32)
    onek = np.zeros((1, D_MODEL), np.float32)
    for h in range(FOX_HEADS):
        for p in range(3):
            selq[p * LANES + h, h * HEAD_DIM + p] = 1.0
            selk[p * LANES + h, h * HEAD_DIM + 3 + p] = -1.0
            oneq[0, h * HEAD_DIM + 3 + p] = 1.0
            onek[0, h * HEAD_DIM + p] = 1.0
    return (jnp.asarray(selq, BF16), jnp.asarray(selk, BF16),
            jnp.asarray(oneq), jnp.asarray(onek))


def _fox_decay(a, w_in_t, layer, b_f):
    b_pad = jnp.pad(b_f, (0, LANES - FOX_HEADS)).reshape(1, LANES)
    selq, selk, oneq, onek = _decay_selectors()
    full = lambda shape: pl.BlockSpec(shape, lambda b: (0, 0))
    out = jax.ShapeDtypeStruct((TOKENS, D_MODEL), BF16)
    gate_block = 4 * D_MODEL // FOX_HEADS
    return pl.pallas_call(
        _fox_decay_kernel,
        grid=(BATCH,),
        in_specs=[pl.BlockSpec((SEQ, D_MODEL), lambda b: (b, 0)),
                  pl.BlockSpec((None, FOX_HEADS, D_MODEL), lambda b: (layer, gate_block, 0)),
                  full((1, LANES)),
                  full((3 * LANES, D_MODEL)), full((3 * LANES, D_MODEL)),
                  full((1, D_MODEL)), full((1, D_MODEL))],
        out_specs=[pl.BlockSpec((SEQ, D_MODEL), lambda b: (b, 0)),
                   pl.BlockSpec((SEQ, D_MODEL), lambda b: (b, 0))],
        out_shape=[out, out],
        scratch_shapes=[pltpu.VMEM((SEQ, LANES), F32)],
        compiler_params=_params("arbitrary"),
        name="fox_decay",
    )(a, w_in_t, b_pad, selq, selk, oneq, onek)


_NT = (((1,), (1,)), ((), ()))


def _transpose_values(v_ref, lanes, vt_ref):
    for c in range(0, SEQ, 512):
        vt_ref[:, c:c + 512] = v_ref[0, c:c + 512, lanes].astype(F32).T.astype(BF16)


def _softmax_columns(sr, pr, n_tiles):
    m = jnp.max(sr[:n_tiles * TK, :], axis=0, keepdims=True)
    l = jnp.zeros((1, TQ), F32)
    for j in range(n_tiles):
        rows = slice(j * TK, (j + 1) * TK)
        p = jnp.exp2(sr[rows, :] - m)
        l = l + jnp.sum(p, axis=0, keepdims=True)
        pr[rows, :] = p.astype(BF16)
    return l


def _staged(n_units, stage_scores, stage_softmax, stage_output):
    for t in range(n_units + 2):
        if t >= 2:
            stage_output(t - 2)
        if 1 <= t <= n_units:
            stage_softmax(t - 1)
        if t < n_units:
            stage_scores(t)


def _fox_attn_kernel(q_ref, cq_ref, k_ref, ck_ref, v_ref, g_ref, o_ref,
                     s_ref, p_ref, vt_ref):
    krow = lax.broadcasted_iota(jnp.int32, (TK, TQ), 0)
    qcol = lax.broadcasted_iota(jnp.int32, (TK, TQ), 1)
    causal = krow <= qcol
    head_lanes = [slice(hh * HEAD_DIM, (hh + 1) * HEAD_DIM) for hh in range(FOX_HPS)]
    for hh in range(FOX_HPS):
        _transpose_values(v_ref, head_lanes[hh], vt_ref.at[hh])
    units = [(hh, i) for i in range(NQ) for hh in range(FOX_HPS)]
    sums = {}

    def stage_scores(u):
        hh, i = units[u]
        hl = head_lanes[hh]
        qs, keys = i * TQ, i * TQ + TK
        sr = s_ref.at[u % FOX_SLOTS]
        qa = jnp.concatenate([q_ref[0, qs:qs + TQ, hl], cq_ref[0, qs:qs + TQ, hl]], axis=-1)
        ka = jnp.concatenate([k_ref[0, :keys, hl], ck_ref[0, :keys, hl]], axis=-1)
        s = lax.dot_general(ka, qa, _NT, preferred_element_type=F32)
        if i > 0:
            sr[:qs, :] = s[:qs]
        sr[qs:keys, :] = jnp.where(causal, s[qs:keys], NEG)

    def stage_softmax(u):
        _, i = units[u]
        sums[u] = _softmax_columns(s_ref.at[u % FOX_SLOTS], p_ref.at[u % FOX_SLOTS], i + 1)

    def stage_output(u):
        hh, i = units[u]
        hl = head_lanes[hh]
        qs, keys = i * TQ, i * TQ + TK
        pr = p_ref.at[u % FOX_SLOTS]
        acc_t = jnp.dot(vt_ref[hh, :, :keys], pr[:keys, :], preferred_element_type=F32)
        out = (acc_t * (1.0 / sums.pop(u))).T
        gate = g_ref[0, qs:qs + TQ, hl].astype(F32)
        o_ref[0, qs:qs + TQ, hl] = (out * gate).astype(BF16)

    _staged(len(units), stage_scores, stage_softmax, stage_output)


def _fox_attention(proj, cq, ck):
    width = FOX_HPS * HEAD_DIM
    groups = FOX_HEADS // FOX_HPS
    blk = lambda off: pl.BlockSpec((1, SEQ, width), lambda b, h: (b, 0, off * groups + h))
    return pl.pallas_call(
        _fox_attn_kernel,
        grid=(BATCH, groups),
        in_specs=[blk(0), blk(0), blk(1), blk(0), blk(2), blk(3)],
        out_specs=blk(0),
        out_shape=jax.ShapeDtypeStruct((BATCH, SEQ, D_MODEL), BF16),
        scratch_shapes=[pltpu.VMEM((FOX_SLOTS, SEQ, TQ), F32),
                        pltpu.VMEM((FOX_SLOTS, SEQ, TQ), BF16),
                        pltpu.VMEM((FOX_HPS, HEAD_DIM, SEQ), BF16)],
        compiler_params=_params("arbitrary", "arbitrary"),
        name="fox_attn",
    )(proj, cq, proj, ck, proj, proj)


def _bias_tile_kernel(rb_ref, o_ref):
    mp = pl.program_id(0)
    krow = lax.broadcasted_iota(jnp.int32, (TK, TQ), 0)
    qcol = lax.broadcasted_iota(jnp.int32, (TK, TQ), 1)
    max_exact = NUM_BUCKETS // 2
    for t in range(2):
        n = jnp.maximum(qcol - krow + t * TK, 0)
        nf = jnp.maximum(n, 1).astype(F32)
        large = max_exact + (jnp.log(nf / max_exact) / math.log(MAX_DISTANCE / max_exact)
                             * (NUM_BUCKETS - max_exact)).astype(jnp.int32)
        large = jnp.minimum(large, NUM_BUCKETS - 1)
        bucket = jnp.where(n < max_exact, n, large)
        val = jnp.zeros((TK, TQ), F32)
        for bkt in range(NUM_BUCKETS):
            val = jnp.where(bucket == bkt, rb_ref[bkt, mp], val)
        val = (val - rb_ref[NUM_BUCKETS - 1, mp]) * LOG2E
        if t == 0:
            val = jnp.where(krow <= qcol, val, NEG)
        o_ref[0, t] = val


def _bias_tiles(rel_bias):
    n_maps = 2 * DIFF_HEADS
    return pl.pallas_call(
        _bias_tile_kernel,
        grid=(n_maps,),
        in_specs=[pl.BlockSpec(memory_space=pltpu.SMEM)],
        out_specs=pl.BlockSpec((1, 2, TK, TQ), lambda mp: (mp, 0, 0, 0)),
        out_shape=jax.ShapeDtypeStruct((n_maps, 2, TK, TQ), F32),
        compiler_params=_params("arbitrary"),
        name="bias_tiles",
    )(rel_bias.reshape(NUM_BUCKETS, n_maps))


def _diff_attn_kernel(q1_ref, q2_ref, k1_ref, k2_ref, v_ref, bias_ref, lam_ref,
                      sub_ref, o_ref, s_ref, p_ref, vt_ref, *, lambda_init):
    lamf = lam_ref[...]
    lam_full = (jnp.exp(jnp.sum(lamf[0:1] * lamf[1:2], axis=-1, keepdims=True))
                - jnp.exp(jnp.sum(lamf[2:3] * lamf[3:4], axis=-1, keepdims=True))
                + lambda_init)
    _transpose_values(v_ref, slice(None), vt_ref)
    maps = ((q1_ref, k1_ref), (q2_ref, k2_ref))
    sums = {}

    def slot(i, mp):
        return 2 * (i % (DIFF_SLOTS // 2)) + mp

    def stage_scores(i):
        qs, keys = i * TQ, i * TQ + TK
        near = qs - TK
        for mp, (q_ref, k_ref) in enumerate(maps):
            sr = s_ref.at[slot(i, mp)]
            s = lax.dot_general(k_ref[0, :keys, :], q_ref[0, qs:qs + TQ, :], _NT,
                                preferred_element_type=F32)
            if i > 1:
                sr[:near, :] = s[:near]
            if i > 0:
                sr[near:qs, :] = s[near:qs] + bias_ref[mp, 1]
            sr[qs:keys, :] = s[qs:keys] + bias_ref[mp, 0]

    def stage_softmax(i):
        for mp in range(2):
            sl = slot(i, mp)
            sums[(i, mp)] = _softmax_columns(s_ref.at[sl], p_ref.at[sl], i + 1)

    def stage_output(i):
        qs, keys = i * TQ, i * TQ + TK
        outs = []
        for mp in range(2):
            pr = p_ref.at[slot(i, mp)]
            acc_t = jnp.dot(vt_ref[:, :keys], pr[:keys, :], preferred_element_type=F32)
            outs.append(acc_t * (1.0 / sums.pop((i, mp))))
        o = (outs[0] - lam_full * outs[1]).T
        ms = jnp.mean(o * o, axis=-1, keepdims=True)
        o = o * lax.rsqrt(ms + SUBLN_EPS) * sub_ref[...] * (1.0 - lambda_init)
        o_ref[0, qs:qs + TQ, :] = o.astype(BF16)

    _staged(NQ, stage_scores, stage_softmax, stage_output)


def _diff_attention(proj, bias_tiles, lam, subln, lambda_init):
    qk = lambda off, j: pl.BlockSpec((1, SEQ, HEAD_DIM),
                                     lambda h, b: (b, 0, off + 2 * h + j))
    n_maps = 2 * DIFF_HEADS
    width = 2 * HEAD_DIM
    return pl.pallas_call(
        functools.partial(_diff_attn_kernel, lambda_init=lambda_init),
        grid=(DIFF_HEADS, BATCH),
        in_specs=[qk(0, 0), qk(0, 1), qk(n_maps, 0), qk(n_maps, 1),
                  pl.BlockSpec((1, SEQ, width), lambda h, b: (b, 0, 2 * DIFF_HEADS + h)),
                  pl.BlockSpec((2, 2, TK, TQ), lambda h, b: (h, 0, 0, 0)),
                  pl.BlockSpec((4, HEAD_DIM), lambda h, b: (0, 0)),
                  pl.BlockSpec((1, width), lambda h, b: (0, 0))],
        out_specs=pl.BlockSpec((1, SEQ, width), lambda h, b: (b, 0, h)),
        out_shape=jax.ShapeDtypeStruct((BATCH, SEQ, D_MODEL), BF16),
        scratch_shapes=[pltpu.VMEM((DIFF_SLOTS, SEQ, TQ), F32),
                        pltpu.VMEM((DIFF_SLOTS, SEQ, TQ), BF16),
                        pltpu.VMEM((width, SEQ), BF16)],
        compiler_params=_params("arbitrary", "arbitrary"),
        name="diff_attn",
    )(proj, proj, proj, proj, proj, bias_tiles, lam, subln.reshape(1, width))


def kernel(x, attn_norm, ffn_norm, final_norm, fox_w_in, fox_b_f, fox_qk_norm, fox_w_out,
           diff_w_in, diff_lambda, diff_subln, diff_w_out, rel_bias,
           ffn_w_in, ffn_conv_w, ffn_conv_b, ffn_w_out):
    h = x.reshape(TOKENS, D_MODEL)
    bias_tiles = _bias_tiles(rel_bias)
    fox_w_in_t = jnp.swapaxes(fox_w_in, 1, 2)
    for i in range(DEPTH):
        j = i // 2
        a = _rmsnorm(h, attn_norm[i], BF16)
        if i % 2 == 0:
            proj = _fox_proj(a, fox_w_in_t, j, fox_qk_norm[j])
            cq, ck = _fox_decay(a, fox_w_in_t, j, fox_b_f[j])
            o = _fox_attention(proj.reshape(BATCH, SEQ, 4 * D_MODEL),
                               cq.reshape(BATCH, SEQ, D_MODEL),
                               ck.reshape(BATCH, SEQ, D_MODEL))
            h, a = _attn_out(o.reshape(TOKENS, D_MODEL), fox_w_out, j, h, ffn_norm[i])
        else:
            lambda_init = 0.8 - 0.6 * math.exp(-0.3 * i)
            proj = _diff_proj(a, diff_w_in, j)
            o = _diff_attention(proj.reshape(BATCH, SEQ, 3 * D_MODEL), bias_tiles,
                                diff_lambda[j], diff_subln[j], lambda_init)
            h, a = _attn_out(o.reshape(TOKENS, D_MODEL), diff_w_out, j, h, ffn_norm[i])
        y = _ffn_in(a, ffn_w_in, i, ffn_conv_w[i], ffn_conv_b[i])
        h = _residual_matmul(y, ffn_w_out, i, h, "ffn_out")
    return _rmsnorm(h, final_norm, F32).reshape(BATCH, SEQ, D_MODEL)
```

```python
import functools
import math

import jax
import jax.numpy as jnp
import numpy as np
from jax import lax
from jax.experimental import pallas as pl
from jax.experimental.pallas import tpu as pltpu

D_MODEL = 2048
BATCH = 4
SEQ = 2048
TOKENS = BATCH * SEQ
DEPTH = 4
HEAD_DIM = 128
FOX_HEADS = D_MODEL // HEAD_DIM
DIFF_HEADS = D_MODEL // (2 * HEAD_DIM)
D_FF = 5632
CONV_WIDTH = 3
NUM_BUCKETS = 32
MAX_DISTANCE = 128
NORM_EPS = 1e-6
SUBLN_EPS = 1e-5

LANES = 128
VMEM_LIMIT = 56 * 1024 * 1024
NEG = -1e30
LOG2E = math.log2(math.e)
Q_SCALE = HEAD_DIM ** -0.5 * LOG2E

TQ = 256
TK = 256
NQ = SEQ // TQ
FOX_HPS = 2
FOX_SLOTS = 5
DIFF_SLOTS = 10

SUB_ROWS = 256
MM_TILES = {
    "fox_proj": (1024, 1024),
    "diff_proj": (1024, 1024),
    "attn_out": (512, D_MODEL),
    "ffn_in": (2048, 512),
    "ffn_out": (512, 1024),
}

F32 = jnp.float32
BF16 = jnp.bfloat16


def _params(*sem):
    return pltpu.CompilerParams(dimension_semantics=sem, vmem_limit_bytes=VMEM_LIMIT)


def _rmsnorm_kernel(x_ref, g_ref, o_ref):
    x = x_ref[...]
    ms = jnp.mean(x * x, axis=-1, keepdims=True)
    o_ref[...] = (x * lax.rsqrt(ms + NORM_EPS) * g_ref[...]).astype(o_ref.dtype)


def _rmsnorm(h, g, out_dtype):
    tm = 512
    return pl.pallas_call(
        _rmsnorm_kernel,
        grid=(TOKENS // tm,),
        in_specs=[pl.BlockSpec((tm, D_MODEL), lambda m: (m, 0)),
                  pl.BlockSpec((1, D_MODEL), lambda m: (0, 0))],
        out_specs=pl.BlockSpec((tm, D_MODEL), lambda m: (m, 0)),
        out_shape=jax.ShapeDtypeStruct((TOKENS, D_MODEL), out_dtype),
        compiler_params=_params("arbitrary"),
        name="rmsnorm",
    )(h, g.reshape(1, D_MODEL))


def _head_rmsnorm(acc, gain, mult):
    ms = jnp.mean(acc * acc, axis=-1, keepdims=True)
    return acc * lax.rsqrt(ms + NORM_EPS) * (gain * mult)


def _row_blocks(tm):
    return range(0, tm, SUB_ROWS)


def _fox_proj_kernel(x_ref, wt_ref, qkn_ref, o_ref, wb_ref, *, tm, tn):
    n = pl.program_id(0)

    @pl.when(pl.program_id(1) == 0)
    def _():
        for c in range(0, tn, SUB_ROWS):
            wb_ref[:, c:c + SUB_ROWS] = wt_ref[c:c + SUB_ROWS, :].T.astype(BF16)

    per = D_MODEL // tn

    def body(epilogue):
        for r in _row_blocks(tm):
            rows = slice(r, r + SUB_ROWS)
            acc = jnp.dot(x_ref[rows, :], wb_ref[...], preferred_element_type=F32)
            epilogue(acc, rows)

    def normed(row, mult):
        gain = qkn_ref[row:row + 1, :]

        def epilogue(acc, rows):
            for c in range(tn // HEAD_DIM):
                sl = slice(c * HEAD_DIM, (c + 1) * HEAD_DIM)
                o_ref[rows, sl] = _head_rmsnorm(acc[:, sl], gain, mult).astype(BF16)
        return epilogue

    def plain(acc, rows):
        o_ref[rows, :] = acc.astype(BF16)

    def gated(acc, rows):
        o_ref[rows, :] = jax.nn.sigmoid(acc).astype(BF16)

    @pl.when(n < per)
    def _():
        body(normed(0, Q_SCALE))

    @pl.when((n >= per) & (n < 2 * per))
    def _():
        body(normed(1, 1.0))

    @pl.when((n >= 2 * per) & (n < 3 * per))
    def _():
        body(plain)

    @pl.when(n >= 3 * per)
    def _():
        body(gated)


def _fox_proj(a, w_in_t, layer, qk_norm):
    tm, tn = MM_TILES["fox_proj"]
    n_out = 4 * D_MODEL
    return pl.pallas_call(
        functools.partial(_fox_proj_kernel, tm=tm, tn=tn),
        grid=(n_out // tn, TOKENS // tm),
        in_specs=[pl.BlockSpec((tm, D_MODEL), lambda n, m: (m, 0)),
                  pl.BlockSpec((None, tn, D_MODEL), lambda n, m: (layer, n, 0)),
                  pl.BlockSpec((2, HEAD_DIM), lambda n, m: (0, 0))],
        out_specs=pl.BlockSpec((tm, tn), lambda n, m: (m, n)),
        out_shape=jax.ShapeDtypeStruct((TOKENS, n_out), BF16),
        scratch_shapes=[pltpu.VMEM((D_MODEL, tn), BF16)],
        compiler_params=_params("arbitrary", "arbitrary"),
        name="fox_proj",
    )(a, w_in_t, qk_norm)


def _diff_proj_kernel(h_ref, g_ref, w_ref, o_ref, wb_ref, *, tm, tn):
    n = pl.program_id(0)

    @pl.when(pl.program_id(1) == 0)
    def _():
        wb_ref[...] = w_ref[...].astype(BF16)

    mult = jnp.where(n < D_MODEL // tn, Q_SCALE, 1.0).astype(F32)
    for r in _row_blocks(tm):
        rows = slice(r, r + SUB_ROWS)
        x = h_ref[rows, :]
        ms = jnp.mean(x * x, axis=-1, keepdims=True)
        a = (x * lax.rsqrt(ms + NORM_EPS) * g_ref[...]).astype(BF16)
        acc = jnp.dot(a, wb_ref[...], preferred_element_type=F32)
        o_ref[rows, :] = (acc * mult).astype(BF16)


def _diff_proj(h, g, w_in, layer):
    tm, tn = MM_TILES["diff_proj"]
    n_out = 3 * D_MODEL
    return pl.pallas_call(
        functools.partial(_diff_proj_kernel, tm=tm, tn=tn),
        grid=(n_out // tn, TOKENS // tm),
        in_specs=[pl.BlockSpec((tm, D_MODEL), lambda n, m: (m, 0)),
                  pl.BlockSpec((1, D_MODEL), lambda n, m: (0, 0)),
                  pl.BlockSpec((None, D_MODEL, tn), lambda n, m: (layer, 0, n))],
        out_specs=pl.BlockSpec((tm, tn), lambda n, m: (m, n)),
        out_shape=jax.ShapeDtypeStruct((TOKENS, n_out), BF16),
        scratch_shapes=[pltpu.VMEM((D_MODEL, tn), BF16)],
        compiler_params=_params("arbitrary", "arbitrary"),
        name="diff_proj",
    )(h, g.reshape(1, D_MODEL), w_in)


def _residual_matmul_kernel(x_ref, w_ref, h_ref, o_ref, wb_ref, *, tm):
    @pl.when(pl.program_id(1) == 0)
    def _():
        wb_ref[...] = w_ref[...].astype(BF16)

    for r in _row_blocks(tm):
        rows = slice(r, r + SUB_ROWS)
        acc = jnp.dot(x_ref[rows, :], wb_ref[...], preferred_element_type=F32)
        o_ref[rows, :] = h_ref[rows, :] + acc


def _residual_matmul(x, w, layer, h, tiles):
    tm, tn = MM_TILES[tiles]
    k = x.shape[1]
    return pl.pallas_call(
        functools.partial(_residual_matmul_kernel, tm=tm),
        grid=(D_MODEL // tn, TOKENS // tm),
        in_specs=[pl.BlockSpec((tm, k), lambda n, m: (m, 0)),
                  pl.BlockSpec((None, k, tn), lambda n, m: (layer, 0, n),
                               pipeline_mode=pl.Buffered(1)),
                  pl.BlockSpec((tm, tn), lambda n, m: (m, n))],
        out_specs=pl.BlockSpec((tm, tn), lambda n, m: (m, n)),
        out_shape=jax.ShapeDtypeStruct((TOKENS, D_MODEL), F32),
        scratch_shapes=[pltpu.VMEM((k, tn), BF16)],
        compiler_params=_params("arbitrary", "arbitrary"),
        name="residual_matmul",
    )(x, w, h)


def _attn_out_kernel(x_ref, w_ref, h_ref, g_ref, o_ref, a_ref, wb_ref, *, tm):
    @pl.when(pl.program_id(0) == 0)
    def _():
        for c in range(0, D_MODEL, 512):
            wb_ref[c:c + 512, :] = w_ref[c:c + 512, :].astype(BF16)

    for r in _row_blocks(tm):
        rows = slice(r, r + SUB_ROWS)
        acc = jnp.dot(x_ref[rows, :], wb_ref[...], preferred_element_type=F32)
        hn = h_ref[rows, :] + acc
        o_ref[rows, :] = hn
        ms = jnp.mean(hn * hn, axis=-1, keepdims=True)
        a_ref[rows, :] = (hn * lax.rsqrt(ms + NORM_EPS) * g_ref[...]).astype(BF16)


def _attn_out(x, w, layer, h, g):
    tm, tn = MM_TILES["attn_out"]
    assert tn == D_MODEL
    rows = lambda m: (m, 0)
    return pl.pallas_call(
        functools.partial(_attn_out_kernel, tm=tm),
        grid=(TOKENS // tm,),
        in_specs=[pl.BlockSpec((tm, D_MODEL), rows),
                  pl.BlockSpec((None, D_MODEL, D_MODEL), lambda m: (layer, 0, 0),
                               pipeline_mode=pl.Buffered(1)),
                  pl.BlockSpec((tm, D_MODEL), rows),
                  pl.BlockSpec((1, D_MODEL), lambda m: (0, 0))],
        out_specs=[pl.BlockSpec((tm, D_MODEL), rows), pl.BlockSpec((tm, D_MODEL), rows)],
        out_shape=[jax.ShapeDtypeStruct((TOKENS, D_MODEL), F32),
                   jax.ShapeDtypeStruct((TOKENS, D_MODEL), BF16)],
        scratch_shapes=[pltpu.VMEM((D_MODEL, D_MODEL), BF16)],
        compiler_params=_params("arbitrary"),
        name="attn_out",
    )(x, w, h, g.reshape(1, D_MODEL))


CARRY_ROWS = 8


def _ffn_in_kernel(x_ref, wg_ref, wu_ref, cw_ref, cb_ref, o_ref,
                   wgb_ref, wub_ref, gs_ref, *, tm):
    m = pl.program_id(1)

    @pl.when(m == 0)
    def _():
        wgb_ref[...] = wg_ref[...].astype(BF16)
        wub_ref[...] = wu_ref[...].astype(BF16)

    @pl.when(m % (SEQ // tm) == 0)
    def _():
        gs_ref[0:CARRY_ROWS, :] = jnp.zeros((CARRY_ROWS, gs_ref.shape[1]), F32)

    for r in _row_blocks(tm):
        x = x_ref[r:r + SUB_ROWS, :]
        g = jnp.dot(x, wgb_ref[...], preferred_element_type=F32)
        u = jnp.dot(x, wub_ref[...], preferred_element_type=F32)
        base = CARRY_ROWS + r
        gs_ref[base:base + SUB_ROWS, :] = g
        g1 = gs_ref[base - 1:base - 1 + SUB_ROWS, :]
        g2 = gs_ref[base - 2:base - 2 + SUB_ROWS, :]
        conv = (cw_ref[2:3, :] * g + cw_ref[1:2, :] * g1 + cw_ref[0:1, :] * g2
                + cb_ref[...])
        gelu = 0.5 * conv * (1.0 + lax.erf(conv * (2.0 ** -0.5)))
        o_ref[r:r + SUB_ROWS, :] = (gelu * u).astype(BF16)
    gs_ref[0:CARRY_ROWS, :] = gs_ref[tm:tm + CARRY_ROWS, :]


def _ffn_in(a, w_in, layer, conv_w, conv_b):
    tm, tn = MM_TILES["ffn_in"]
    nt = D_FF // tn
    return pl.pallas_call(
        functools.partial(_ffn_in_kernel, tm=tm),
        grid=(nt, TOKENS // tm),
        in_specs=[pl.BlockSpec((tm, D_MODEL), lambda n, m: (m, 0)),
                  pl.BlockSpec((None, D_MODEL, tn), lambda n, m: (layer, 0, n)),
                  pl.BlockSpec((None, D_MODEL, tn), lambda n, m: (layer, 0, nt + n)),
                  pl.BlockSpec((CONV_WIDTH, tn), lambda n, m: (0, n)),
                  pl.BlockSpec((1, tn), lambda n, m: (0, n))],
        out_specs=pl.BlockSpec((tm, tn), lambda n, m: (m, n)),
        out_shape=jax.ShapeDtypeStruct((TOKENS, D_FF), BF16),
        scratch_shapes=[pltpu.VMEM((D_MODEL, tn), BF16),
                        pltpu.VMEM((D_MODEL, tn), BF16),
                        pltpu.VMEM((tm + CARRY_ROWS, tn), F32)],
        compiler_params=_params("arbitrary", "arbitrary"),
        name="ffn_in",
    )(a, w_in, w_in, conv_w, conv_b.reshape(1, D_FF))


def _split3(x):
    hi = x.astype(BF16)
    r1 = x - hi.astype(F32)
    mid = r1.astype(BF16)
    lo = (r1 - mid.astype(F32)).astype(BF16)
    return jnp.concatenate([hi, mid, lo], axis=-1)


def _split3_packed(c):
    hi = c.astype(BF16).astype(F32)
    r1 = c - hi
    mid = r1.astype(BF16).astype(F32)
    lo = r1 - mid
    lane = lax.broadcasted_iota(jnp.int32, c.shape, 1)
    packed = jnp.where(lane < FOX_HEADS, hi,
                       jnp.where(lane < 2 * FOX_HEADS, pltpu.roll(mid, FOX_HEADS, 1),
                                 jnp.where(lane < 3 * FOX_HEADS,
                                           pltpu.roll(lo, 2 * FOX_HEADS, 1), 0.0)))
    return packed.astype(BF16)


CUM_BLOCK = 512


DECAY_ROWS = 1024


def _fox_decay_kernel(h_ref, g_ref, wt_ref, b_ref, selq_ref, selk_ref, oneq_ref, onek_ref,
                      a_ref, cq_ref, ck_ref, c_ref, carry_ref):
    @pl.when(pl.program_id(1) == 0)
    def _():
        carry_ref[...] = jnp.zeros(carry_ref.shape, F32)

    wt = jnp.concatenate([wt_ref[...], jnp.zeros((LANES - FOX_HEADS, D_MODEL), F32)],
                         axis=0).astype(BF16)
    row = lax.broadcasted_iota(jnp.int32, (CUM_BLOCK, CUM_BLOCK), 0)
    col = lax.broadcasted_iota(jnp.int32, (CUM_BLOCK, CUM_BLOCK), 1)
    tri = jnp.where(row >= col, 1.0, 0.0).astype(BF16)
    carry = carry_ref[0:1, :]
    for blk in range(DECAY_ROWS // CUM_BLOCK):
        sl = slice(blk * CUM_BLOCK, (blk + 1) * CUM_BLOCK)
        x = h_ref[sl, :]
        ms = jnp.mean(x * x, axis=-1, keepdims=True)
        a = (x * lax.rsqrt(ms + NORM_EPS) * g_ref[...]).astype(BF16)
        a_ref[sl, :] = a
        z = lax.dot_general(a, wt, _NT, preferred_element_type=F32) + b_ref[...]
        logf = (jnp.minimum(z, 0.0) - jnp.log1p(jnp.exp(-jnp.abs(z)))) * LOG2E
        cs = jnp.dot(tri, _split3(logf), preferred_element_type=F32)
        c_blk = (cs[:, :LANES] + cs[:, LANES:2 * LANES] + cs[:, 2 * LANES:]) + carry
        carry = c_blk[CUM_BLOCK - 1:CUM_BLOCK, :]
        c_ref[sl, :] = c_blk
    carry_ref[0:1, :] = carry
    parts = _split3_packed(c_ref[...])
    cq = jnp.dot(parts, selq_ref[...], preferred_element_type=F32) + oneq_ref[...]
    ck = jnp.dot(parts, selk_ref[...], preferred_element_type=F32) + onek_ref[...]
    cq_ref[...] = cq.astype(BF16)
    ck_ref[...] = ck.astype(BF16)


def _decay_selectors():
    selq = np.zeros((LANES, D_MODEL), np.float32)
    selk = np.zeros((LANES, D_MODEL), np.float32)
    oneq = np.zeros((1, D_MODEL), np.float32)
    onek = np.zeros((1, D_MODEL), np.float32)
    for h in range(FOX_HEADS):
        for p in range(3):
            selq[p * FOX_HEADS + h, h * HEAD_DIM + p] = 1.0
            selk[p * FOX_HEADS + h, h * HEAD_DIM + 3 + p] = -1.0
            oneq[0, h * HEAD_DIM + 3 + p] = 1.0
            onek[0, h * HEAD_DIM + p] = 1.0
    return (jnp.asarray(selq, BF16), jnp.asarray(selk, BF16),
            jnp.asarray(oneq), jnp.asarray(onek))


def _fox_norm_decay(h, g, w_in_t, layer, b_f):
    b_pad = jnp.pad(b_f, (0, LANES - FOX_HEADS)).reshape(1, LANES)
    selq, selk, oneq, onek = _decay_selectors()
    steps = SEQ // DECAY_ROWS
    full = lambda shape: pl.BlockSpec(shape, lambda b, s: (0, 0))
    rows = pl.BlockSpec((DECAY_ROWS, D_MODEL), lambda b, s: (b * steps + s, 0))
    out = jax.ShapeDtypeStruct((TOKENS, D_MODEL), BF16)
    gate_block = 4 * D_MODEL // FOX_HEADS
    return pl.pallas_call(
        _fox_decay_kernel,
        grid=(BATCH, steps),
        in_specs=[rows, full((1, D_MODEL)),
                  pl.BlockSpec((None, FOX_HEADS, D_MODEL),
                               lambda b, s: (layer, gate_block, 0)),
                  full((1, LANES)),
                  full((LANES, D_MODEL)), full((LANES, D_MODEL)),
                  full((1, D_MODEL)), full((1, D_MODEL))],
        out_specs=[rows, rows, rows],
        out_shape=[out, out, out],
        scratch_shapes=[pltpu.VMEM((DECAY_ROWS, LANES), F32),
                        pltpu.VMEM((8, LANES), F32)],
        compiler_params=_params("arbitrary", "arbitrary"),
        name="fox_decay",
    )(h, g.reshape(1, D_MODEL), w_in_t, b_pad, selq, selk, oneq, onek)


_NT = (((1,), (1,)), ((), ()))


def _transpose_values(v_ref, lanes, vt_ref):
    for c in range(0, SEQ, 512):
        vt_ref[:, c:c + 512] = v_ref[0, c:c + 512, lanes].astype(F32).T.astype(BF16)


def _softmax_columns(sr, pr, n_tiles):
    m = jnp.max(sr[:n_tiles * TK, :], axis=0, keepdims=True)
    l = jnp.zeros((1, TQ), F32)
    for j in range(n_tiles):
        rows = slice(j * TK, (j + 1) * TK)
        p = jnp.exp2(sr[rows, :] - m)
        l = l + jnp.sum(p, axis=0, keepdims=True)
        pr[rows, :] = p.astype(BF16)
    return l


def _staged(n_units, stage_scores, stage_softmax, stage_output):
    stages = ((stage_scores, 0), (stage_softmax, 2), (stage_output, 4))
    for t in range(n_units + 4):
        for stage, lag in stages:
            if 0 <= t - lag < n_units:
                stage(t - lag)


def _fox_attn_kernel(q_ref, cq_ref, k_ref, ck_ref, v_ref, g_ref, o_ref,
                     s_ref, p_ref, vt_ref):
    krow = lax.broadcasted_iota(jnp.int32, (TK, TQ), 0)
    qcol = lax.broadcasted_iota(jnp.int32, (TK, TQ), 1)
    causal = krow <= qcol
    head_lanes = [slice(hh * HEAD_DIM, (hh + 1) * HEAD_DIM) for hh in range(FOX_HPS)]
    for hh in range(FOX_HPS):
        _transpose_values(v_ref, head_lanes[hh], vt_ref.at[hh])
    units = [(hh, i) for i in range(NQ) for hh in range(FOX_HPS)]
    sums = {}

    def stage_scores(u):
        hh, i = units[u]
        hl = head_lanes[hh]
        qs, keys = i * TQ, i * TQ + TK
        sr = s_ref.at[u % FOX_SLOTS]
        qa = jnp.concatenate([q_ref[0, qs:qs + TQ, hl], cq_ref[0, qs:qs + TQ, hl]], axis=-1)
        ka = jnp.concatenate([k_ref[0, :keys, hl], ck_ref[0, :keys, hl]], axis=-1)
        s = lax.dot_general(ka, qa, _NT, preferred_element_type=F32)
        if i > 0:
            sr[:qs, :] = s[:qs]
        sr[qs:keys, :] = jnp.where(causal, s[qs:keys], NEG)

    def stage_softmax(u):
        _, i = units[u]
        sums[u] = _softmax_columns(s_ref.at[u % FOX_SLOTS], p_ref.at[u % FOX_SLOTS], i + 1)

    def stage_output(u):
        hh, i = units[u]
        hl = head_lanes[hh]
        qs, keys = i * TQ, i * TQ + TK
        pr = p_ref.at[u % FOX_SLOTS]
        acc_t = jnp.dot(vt_ref[hh, :, :keys], pr[:keys, :], preferred_element_type=F32)
        out = (acc_t * (1.0 / sums.pop(u))).T
        gate = g_ref[0, qs:qs + TQ, hl].astype(F32)
        o_ref[0, qs:qs + TQ, hl] = (out * gate).astype(BF16)

    _staged(len(units), stage_scores, stage_softmax, stage_output)


def _fox_attention(proj, cq, ck):
    width = FOX_HPS * HEAD_DIM
    groups = FOX_HEADS // FOX_HPS
    blk = lambda off: pl.BlockSpec((1, SEQ, width), lambda b, h: (b, 0, off * groups + h))
    return pl.pallas_call(
        _fox_attn_kernel,
        grid=(BATCH, groups),
        in_specs=[blk(0), blk(0), blk(1), blk(0), blk(2), blk(3)],
        out_specs=blk(0),
        out_shape=jax.ShapeDtypeStruct((BATCH, SEQ, D_MODEL), BF16),
        scratch_shapes=[pltpu.VMEM((FOX_SLOTS, SEQ, TQ), F32),
                        pltpu.VMEM((FOX_SLOTS, SEQ, TQ), BF16),
                        pltpu.VMEM((FOX_HPS, HEAD_DIM, SEQ), BF16)],
        compiler_params=_params("arbitrary", "arbitrary"),
        name="fox_attn",
    )(proj, cq, proj, ck, proj, proj)


def _bias_tile_kernel(rb_ref, o_ref):
    mp = pl.program_id(0)
    krow = lax.broadcasted_iota(jnp.int32, (TK, TQ), 0)
    qcol = lax.broadcasted_iota(jnp.int32, (TK, TQ), 1)
    max_exact = NUM_BUCKETS // 2
    for t in range(2):
        n = jnp.maximum(qcol - krow + t * TK, 0)
        nf = jnp.maximum(n, 1).astype(F32)
        large = max_exact + (jnp.log(nf / max_exact) / math.log(MAX_DISTANCE / max_exact)
                             * (NUM_BUCKETS - max_exact)).astype(jnp.int32)
        large = jnp.minimum(large, NUM_BUCKETS - 1)
        bucket = jnp.where(n < max_exact, n, large)
        val = jnp.zeros((TK, TQ), F32)
        for bkt in range(NUM_BUCKETS):
            val = jnp.where(bucket == bkt, rb_ref[bkt, mp], val)
        val = (val - rb_ref[NUM_BUCKETS - 1, mp]) * LOG2E
        if t == 0:
            val = jnp.where(krow <= qcol, val, NEG)
        o_ref[0, t] = val


def _bias_tiles(rel_bias):
    n_maps = 2 * DIFF_HEADS
    return pl.pallas_call(
        _bias_tile_kernel,
        grid=(n_maps,),
        in_specs=[pl.BlockSpec(memory_space=pltpu.SMEM)],
        out_specs=pl.BlockSpec((1, 2, TK, TQ), lambda mp: (mp, 0, 0, 0)),
        out_shape=jax.ShapeDtypeStruct((n_maps, 2, TK, TQ), F32),
        compiler_params=_params("arbitrary"),
        name="bias_tiles",
    )(rel_bias.reshape(NUM_BUCKETS, n_maps))


def _diff_attn_kernel(q1_ref, q2_ref, k1_ref, k2_ref, v_ref, bias_ref, lam_ref,
                      sub_ref, o_ref, s_ref, p_ref, vt_ref, *, lambda_init):
    lamf = lam_ref[...]
    lam_full = (jnp.exp(jnp.sum(lamf[0:1] * lamf[1:2], axis=-1, keepdims=True))
                - jnp.exp(jnp.sum(lamf[2:3] * lamf[3:4], axis=-1, keepdims=True))
                + lambda_init)
    _transpose_values(v_ref, slice(None), vt_ref)
    maps = ((q1_ref, k1_ref), (q2_ref, k2_ref))
    sums = {}

    def slot(i, mp):
        return 2 * (i % (DIFF_SLOTS // 2)) + mp

    def stage_scores(i):
        qs, keys = i * TQ, i * TQ + TK
        near = qs - TK
        for mp, (q_ref, k_ref) in enumerate(maps):
            sr = s_ref.at[slot(i, mp)]
            s = lax.dot_general(k_ref[0, :keys, :], q_ref[0, qs:qs + TQ, :], _NT,
                                preferred_element_type=F32)
            if i > 1:
                sr[:near, :] = s[:near]
            if i > 0:
                sr[near:qs, :] = s[near:qs] + bias_ref[mp, 1]
            sr[qs:keys, :] = s[qs:keys] + bias_ref[mp, 0]

    def stage_softmax(i):
        for mp in range(2):
            sl = slot(i, mp)
            sums[(i, mp)] = _softmax_columns(s_ref.at[sl], p_ref.at[sl], i + 1)

    def stage_output(i):
        qs, keys = i * TQ, i * TQ + TK
        outs = []
        for mp in range(2):
            pr = p_ref.at[slot(i, mp)]
            acc_t = jnp.dot(vt_ref[:, :keys], pr[:keys, :], preferred_element_type=F32)
            outs.append(acc_t * (1.0 / sums.pop((i, mp))))
        o = (outs[0] - lam_full * outs[1]).T
        ms = jnp.mean(o * o, axis=-1, keepdims=True)
        o = o * lax.rsqrt(ms + SUBLN_EPS) * sub_ref[...] * (1.0 - lambda_init)
        o_ref[0, qs:qs + TQ, :] = o.astype(BF16)

    _staged(NQ, stage_scores, stage_softmax, stage_output)


def _diff_attention(proj, bias_tiles, lam, subln, lambda_init):
    qk = lambda off, j: pl.BlockSpec((1, SEQ, HEAD_DIM),
                                     lambda h, b: (b, 0, off + 2 * h + j))
    n_maps = 2 * DIFF_HEADS
    width = 2 * HEAD_DIM
    return pl.pallas_call(
        functools.partial(_diff_attn_kernel, lambda_init=lambda_init),
        grid=(DIFF_HEADS, BATCH),
        in_specs=[qk(0, 0), qk(0, 1), qk(n_maps, 0), qk(n_maps, 1),
                  pl.BlockSpec((1, SEQ, width), lambda h, b: (b, 0, 2 * DIFF_HEADS + h)),
                  pl.BlockSpec((2, 2, TK, TQ), lambda h, b: (h, 0, 0, 0)),
                  pl.BlockSpec((4, HEAD_DIM), lambda h, b: (0, 0)),
                  pl.BlockSpec((1, width), lambda h, b: (0, 0))],
        out_specs=pl.BlockSpec((1, SEQ, width), lambda h, b: (b, 0, h)),
        out_shape=jax.ShapeDtypeStruct((BATCH, SEQ, D_MODEL), BF16),
        scratch_shapes=[pltpu.VMEM((DIFF_SLOTS, SEQ, TQ), F32),
                        pltpu.VMEM((DIFF_SLOTS, SEQ, TQ), BF16),
                        pltpu.VMEM((width, SEQ), BF16)],
        compiler_params=_params("arbitrary", "arbitrary"),
        name="diff_attn",
    )(proj, proj, proj, proj, proj, bias_tiles, lam, subln.reshape(1, width))


def kernel(x, attn_norm, ffn_norm, final_norm, fox_w_in, fox_b_f, fox_qk_norm, fox_w_out,
           diff_w_in, diff_lambda, diff_subln, diff_w_out, rel_bias,
           ffn_w_in, ffn_conv_w, ffn_conv_b, ffn_w_out):
    h = x.reshape(TOKENS, D_MODEL)
    bias_tiles = _bias_tiles(rel_bias)
    fox_w_in_t = jnp.swapaxes(fox_w_in, 1, 2)
    for i in range(DEPTH):
        j = i // 2
        if i % 2 == 0:
            a, cq, ck = _fox_norm_decay(h, attn_norm[i], fox_w_in_t, j, fox_b_f[j])
            proj = _fox_proj(a, fox_w_in_t, j, fox_qk_norm[j])
            o = _fox_attention(proj.reshape(BATCH, SEQ, 4 * D_MODEL),
                               cq.reshape(BATCH, SEQ, D_MODEL),
                               ck.reshape(BATCH, SEQ, D_MODEL))
            h, a = _attn_out(o.reshape(TOKENS, D_MODEL), fox_w_out, j, h, ffn_norm[i])
        else:
            lambda_init = 0.8 - 0.6 * math.exp(-0.3 * i)
            proj = _diff_proj(h, attn_norm[i], diff_w_in, j)
            o = _diff_attention(proj.reshape(BATCH, SEQ, 3 * D_MODEL), bias_tiles,
                                diff_lambda[j], diff_subln[j], lambda_init)
            h, a = _attn_out(o.reshape(TOKENS, D_MODEL), diff_w_out, j, h, ffn_norm[i])
        y = _ffn_in(a, ffn_w_in, i, ffn_conv_w[i], ffn_conv_b[i])
        h = _residual_matmul(y, ffn_w_out, i, h, "ffn_out")
    return _rmsnorm(h, final_norm, F32).reshape(BATCH, SEQ, D_MODEL)
```

```python
import functools
import math

import jax
import jax.numpy as jnp
import numpy as np
from jax import lax
from jax.experimental import pallas as pl
from jax.experimental.pallas import tpu as pltpu

D_MODEL = 2048
BATCH = 4
SEQ = 2048
TOKENS = BATCH * SEQ
DEPTH = 4
HEAD_DIM = 128
FOX_HEADS = D_MODEL // HEAD_DIM
DIFF_HEADS = D_MODEL // (2 * HEAD_DIM)
D_FF = 5632
CONV_WIDTH = 3
NUM_BUCKETS = 32
MAX_DISTANCE = 128
NORM_EPS = 1e-6
SUBLN_EPS = 1e-5

LANES = 128
VMEM_LIMIT = 56 * 1024 * 1024
NEG = -1e30
LOG2E = math.log2(math.e)
Q_SCALE = HEAD_DIM ** -0.5 * LOG2E

TQ = 256
TK = 256
NQ = SEQ // TQ
FOX_HPS = 2
FOX_SLOTS = 5
DIFF_SLOTS = 10

SUB_ROWS = 256
MM_TILES = {
    "fox_proj": (1024, 1024),
    "diff_proj": (1024, 1024),
    "attn_out": (512, D_MODEL),
    "ffn_in": (2048, 512),
    "ffn_out": (512, 1024),
}

F32 = jnp.float32
BF16 = jnp.bfloat16


def _params(*sem):
    return pltpu.CompilerParams(dimension_semantics=sem, vmem_limit_bytes=VMEM_LIMIT)


def _rmsnorm_kernel(x_ref, g_ref, o_ref):
    x = x_ref[...]
    ms = jnp.mean(x * x, axis=-1, keepdims=True)
    o_ref[...] = (x * lax.rsqrt(ms + NORM_EPS) * g_ref[...]).astype(o_ref.dtype)


def _rmsnorm(h, g, out_dtype):
    tm = 512
    return pl.pallas_call(
        _rmsnorm_kernel,
        grid=(TOKENS // tm,),
        in_specs=[pl.BlockSpec((tm, D_MODEL), lambda m: (m, 0)),
                  pl.BlockSpec((1, D_MODEL), lambda m: (0, 0))],
        out_specs=pl.BlockSpec((tm, D_MODEL), lambda m: (m, 0)),
        out_shape=jax.ShapeDtypeStruct((TOKENS, D_MODEL), out_dtype),
        compiler_params=_params("arbitrary"),
        name="rmsnorm",
    )(h, g.reshape(1, D_MODEL))


def _head_rmsnorm(acc, gain, mult):
    ms = jnp.mean(acc * acc, axis=-1, keepdims=True)
    return acc * lax.rsqrt(ms + NORM_EPS) * (gain * mult)


def _row_blocks(tm):
    return range(0, tm, SUB_ROWS)


def _fox_proj_kernel(x_ref, wt_ref, qkn_ref, o_ref, wb_ref, *, tm, tn):
    n = pl.program_id(0)

    @pl.when(pl.program_id(1) == 0)
    def _():
        for c in range(0, tn, SUB_ROWS):
            wb_ref[:, c:c + SUB_ROWS] = wt_ref[c:c + SUB_ROWS, :].T.astype(BF16)

    per = D_MODEL // tn

    def body(epilogue):
        for r in _row_blocks(tm):
            rows = slice(r, r + SUB_ROWS)
            acc = jnp.dot(x_ref[rows, :], wb_ref[...], preferred_element_type=F32)
            epilogue(acc, rows)

    def normed(row, mult):
        gain = qkn_ref[row:row + 1, :]

        def epilogue(acc, rows):
            for c in range(tn // HEAD_DIM):
                sl = slice(c * HEAD_DIM, (c + 1) * HEAD_DIM)
                o_ref[rows, sl] = _head_rmsnorm(acc[:, sl], gain, mult).astype(BF16)
        return epilogue

    def plain(acc, rows):
        o_ref[rows, :] = acc.astype(BF16)

    def gated(acc, rows):
        o_ref[rows, :] = jax.nn.sigmoid(acc).astype(BF16)

    @pl.when(n < per)
    def _():
        body(normed(0, Q_SCALE))

    @pl.when((n >= per) & (n < 2 * per))
    def _():
        body(normed(1, 1.0))

    @pl.when((n >= 2 * per) & (n < 3 * per))
    def _():
        body(plain)

    @pl.when(n >= 3 * per)
    def _():
        body(gated)


def _fox_proj(a, w_in_t, layer, qk_norm):
    tm, tn = MM_TILES["fox_proj"]
    n_out = 4 * D_MODEL
    return pl.pallas_call(
        functools.partial(_fox_proj_kernel, tm=tm, tn=tn),
        grid=(n_out // tn, TOKENS // tm),
        in_specs=[pl.BlockSpec((tm, D_MODEL), lambda n, m: (m, 0)),
                  pl.BlockSpec((None, tn, D_MODEL), lambda n, m: (layer, n, 0)),
                  pl.BlockSpec((2, HEAD_DIM), lambda n, m: (0, 0))],
        out_specs=pl.BlockSpec((tm, tn), lambda n, m: (m, n)),
        out_shape=jax.ShapeDtypeStruct((TOKENS, n_out), BF16),
        scratch_shapes=[pltpu.VMEM((D_MODEL, tn), BF16)],
        compiler_params=_params("arbitrary", "arbitrary"),
        name="fox_proj",
    )(a, w_in_t, qk_norm)


def _diff_proj_kernel(h_ref, g_ref, w_ref, o_ref, wb_ref, *, tm, tn):
    n = pl.program_id(0)

    @pl.when(pl.program_id(1) == 0)
    def _():
        wb_ref[...] = w_ref[...].astype(BF16)

    mult = jnp.where(n < D_MODEL // tn, Q_SCALE, 1.0).astype(F32)
    for r in _row_blocks(tm):
        rows = slice(r, r + SUB_ROWS)
        x = h_ref[rows, :]
        ms = jnp.mean(x * x, axis=-1, keepdims=True)
        a = (x * lax.rsqrt(ms + NORM_EPS) * g_ref[...]).astype(BF16)
        acc = jnp.dot(a, wb_ref[...], preferred_element_type=F32)
        o_ref[rows, :] = (acc * mult).astype(BF16)


def _diff_proj(h, g, w_in, layer):
    tm, tn = MM_TILES["diff_proj"]
    n_out = 3 * D_MODEL
    return pl.pallas_call(
        functools.partial(_diff_proj_kernel, tm=tm, tn=tn),
        grid=(n_out // tn, TOKENS // tm),
        in_specs=[pl.BlockSpec((tm, D_MODEL), lambda n, m: (m, 0)),
                  pl.BlockSpec((1, D_MODEL), lambda n, m: (0, 0)),
                  pl.BlockSpec((None, D_MODEL, tn), lambda n, m: (layer, 0, n))],
        out_specs=pl.BlockSpec((tm, tn), lambda n, m: (m, n)),
        out_shape=jax.ShapeDtypeStruct((TOKENS, n_out), BF16),
        scratch_shapes=[pltpu.VMEM((D_MODEL, tn), BF16)],
        compiler_params=_params("arbitrary", "arbitrary"),
        name="diff_proj",
    )(h, g.reshape(1, D_MODEL), w_in)


def _residual_matmul_kernel(x_ref, w_ref, h_ref, o_ref, wb_ref, *, tm):
    @pl.when(pl.program_id(1) == 0)
    def _():
        wb_ref[...] = w_ref[...].astype(BF16)

    for r in _row_blocks(tm):
        rows = slice(r, r + SUB_ROWS)
        acc = jnp.dot(x_ref[rows, :], wb_ref[...], preferred_element_type=F32)
        o_ref[rows, :] = h_ref[rows, :] + acc


def _residual_matmul(x, w, layer, h, tiles):
    tm, tn = MM_TILES[tiles]
    k = x.shape[1]
    return pl.pallas_call(
        functools.partial(_residual_matmul_kernel, tm=tm),
        grid=(D_MODEL // tn, TOKENS // tm),
        in_specs=[pl.BlockSpec((tm, k), lambda n, m: (m, 0)),
                  pl.BlockSpec((None, k, tn), lambda n, m: (layer, 0, n),
                               pipeline_mode=pl.Buffered(1)),
                  pl.BlockSpec((tm, tn), lambda n, m: (m, n))],
        out_specs=pl.BlockSpec((tm, tn), lambda n, m: (m, n)),
        out_shape=jax.ShapeDtypeStruct((TOKENS, D_MODEL), F32),
        scratch_shapes=[pltpu.VMEM((k, tn), BF16)],
        compiler_params=_params("arbitrary", "arbitrary"),
        name="residual_matmul",
    )(x, w, h)


def _attn_out_kernel(x_ref, w_ref, h_ref, g_ref, o_ref, a_ref, wb_ref, *, tm):
    @pl.when(pl.program_id(0) == 0)
    def _():
        for c in range(0, D_MODEL, 512):
            wb_ref[c:c + 512, :] = w_ref[c:c + 512, :].astype(BF16)

    for r in _row_blocks(tm):
        rows = slice(r, r + SUB_ROWS)
        acc = jnp.dot(x_ref[rows, :], wb_ref[...], preferred_element_type=F32)
        hn = h_ref[rows, :] + acc
        o_ref[rows, :] = hn
        ms = jnp.mean(hn * hn, axis=-1, keepdims=True)
        a_ref[rows, :] = (hn * lax.rsqrt(ms + NORM_EPS) * g_ref[...]).astype(BF16)


def _attn_out(x, w, layer, h, g):
    tm, tn = MM_TILES["attn_out"]
    assert tn == D_MODEL
    rows = lambda m: (m, 0)
    return pl.pallas_call(
        functools.partial(_attn_out_kernel, tm=tm),
        grid=(TOKENS // tm,),
        in_specs=[pl.BlockSpec((tm, D_MODEL), rows),
                  pl.BlockSpec((None, D_MODEL, D_MODEL), lambda m: (layer, 0, 0),
                               pipeline_mode=pl.Buffered(1)),
                  pl.BlockSpec((tm, D_MODEL), rows),
                  pl.BlockSpec((1, D_MODEL), lambda m: (0, 0))],
        out_specs=[pl.BlockSpec((tm, D_MODEL), rows), pl.BlockSpec((tm, D_MODEL), rows)],
        out_shape=[jax.ShapeDtypeStruct((TOKENS, D_MODEL), F32),
                   jax.ShapeDtypeStruct((TOKENS, D_MODEL), BF16)],
        scratch_shapes=[pltpu.VMEM((D_MODEL, D_MODEL), BF16)],
        compiler_params=_params("arbitrary"),
        name="attn_out",
    )(x, w, h, g.reshape(1, D_MODEL))


CARRY_ROWS = 8


def _ffn_in_kernel(x_ref, wg_ref, wu_ref, cw_ref, cb_ref, o_ref,
                   wgb_ref, wub_ref, gs_ref, *, tm):
    m = pl.program_id(1)

    @pl.when(m == 0)
    def _():
        wgb_ref[...] = wg_ref[...].astype(BF16)
        wub_ref[...] = wu_ref[...].astype(BF16)

    @pl.when(m % (SEQ // tm) == 0)
    def _():
        gs_ref[0:CARRY_ROWS, :] = jnp.zeros((CARRY_ROWS, gs_ref.shape[1]), F32)

    for r in _row_blocks(tm):
        x = x_ref[r:r + SUB_ROWS, :]
        g = jnp.dot(x, wgb_ref[...], preferred_element_type=F32)
        u = jnp.dot(x, wub_ref[...], preferred_element_type=F32)
        base = CARRY_ROWS + r
        gs_ref[base:base + SUB_ROWS, :] = g
        g1 = gs_ref[base - 1:base - 1 + SUB_ROWS, :]
        g2 = gs_ref[base - 2:base - 2 + SUB_ROWS, :]
        conv = (cw_ref[2:3, :] * g + cw_ref[1:2, :] * g1 + cw_ref[0:1, :] * g2
                + cb_ref[...])
        gelu = 0.5 * conv * (1.0 + lax.erf(conv * (2.0 ** -0.5)))
        o_ref[r:r + SUB_ROWS, :] = (gelu * u).astype(BF16)
    gs_ref[0:CARRY_ROWS, :] = gs_ref[tm:tm + CARRY_ROWS, :]


def _ffn_in(a, w_in, layer, conv_w, conv_b):
    tm, tn = MM_TILES["ffn_in"]
    nt = D_FF // tn
    return pl.pallas_call(
        functools.partial(_ffn_in_kernel, tm=tm),
        grid=(nt, TOKENS // tm),
        in_specs=[pl.BlockSpec((tm, D_MODEL), lambda n, m: (m, 0)),
                  pl.BlockSpec((None, D_MODEL, tn), lambda n, m: (layer, 0, n)),
                  pl.BlockSpec((None, D_MODEL, tn), lambda n, m: (layer, 0, nt + n)),
                  pl.BlockSpec((CONV_WIDTH, tn), lambda n, m: (0, n)),
                  pl.BlockSpec((1, tn), lambda n, m: (0, n))],
        out_specs=pl.BlockSpec((tm, tn), lambda n, m: (m, n)),
        out_shape=jax.ShapeDtypeStruct((TOKENS, D_FF), BF16),
        scratch_shapes=[pltpu.VMEM((D_MODEL, tn), BF16),
                        pltpu.VMEM((D_MODEL, tn), BF16),
                        pltpu.VMEM((tm + CARRY_ROWS, tn), F32)],
        compiler_params=_params("arbitrary", "arbitrary"),
        name="ffn_in",
    )(a, w_in, w_in, conv_w, conv_b.reshape(1, D_FF))


def _split3(x):
    hi = x.astype(BF16)
    r1 = x - hi.astype(F32)
    mid = r1.astype(BF16)
    lo = (r1 - mid.astype(F32)).astype(BF16)
    return jnp.concatenate([hi, mid, lo], axis=-1)


def _split3_packed(c):
    hi = c.astype(BF16).astype(F32)
    r1 = c - hi
    mid = r1.astype(BF16).astype(F32)
    lo = r1 - mid
    lane = lax.broadcasted_iota(jnp.int32, c.shape, 1)
    packed = jnp.where(lane < FOX_HEADS, hi,
                       jnp.where(lane < 2 * FOX_HEADS, pltpu.roll(mid, FOX_HEADS, 1),
                                 jnp.where(lane < 3 * FOX_HEADS,
                                           pltpu.roll(lo, 2 * FOX_HEADS, 1), 0.0)))
    return packed.astype(BF16)


CUM_BLOCK = 512


DECAY_ROWS = 1024


def _fox_decay_kernel(h_ref, g_ref, wt_ref, b_ref, selq_ref, selk_ref, oneq_ref, onek_ref,
                      a_ref, cq_ref, ck_ref, c_ref, carry_ref):
    @pl.when(pl.program_id(1) == 0)
    def _():
        carry_ref[...] = jnp.zeros(carry_ref.shape, F32)

    wt = jnp.concatenate([wt_ref[...], jnp.zeros((LANES - FOX_HEADS, D_MODEL), F32)],
                         axis=0).astype(BF16)
    row = lax.broadcasted_iota(jnp.int32, (CUM_BLOCK, CUM_BLOCK), 0)
    col = lax.broadcasted_iota(jnp.int32, (CUM_BLOCK, CUM_BLOCK), 1)
    tri = jnp.where(row >= col, 1.0, 0.0).astype(BF16)
    carry = carry_ref[0:1, :]
    for blk in range(DECAY_ROWS // CUM_BLOCK):
        sl = slice(blk * CUM_BLOCK, (blk + 1) * CUM_BLOCK)
        x = h_ref[sl, :]
        ms = jnp.mean(x * x, axis=-1, keepdims=True)
        a = (x * lax.rsqrt(ms + NORM_EPS) * g_ref[...]).astype(BF16)
        a_ref[sl, :] = a
        z = lax.dot_general(a, wt, _NT, preferred_element_type=F32) + b_ref[...]
        logf = (jnp.minimum(z, 0.0) - jnp.log1p(jnp.exp(-jnp.abs(z)))) * LOG2E
        cs = jnp.dot(tri, _split3(logf), preferred_element_type=F32)
        c_blk = (cs[:, :LANES] + cs[:, LANES:2 * LANES] + cs[:, 2 * LANES:]) + carry
        carry = c_blk[CUM_BLOCK - 1:CUM_BLOCK, :]
        c_ref[sl, :] = c_blk
    carry_ref[0:1, :] = carry
    parts = _split3_packed(c_ref[...])
    cq = jnp.dot(parts, selq_ref[...], preferred_element_type=F32) + oneq_ref[...]
    ck = jnp.dot(parts, selk_ref[...], preferred_element_type=F32) + onek_ref[...]
    cq_ref[...] = cq.astype(BF16)
    ck_ref[...] = ck.astype(BF16)


def _decay_selectors():
    selq = np.zeros((LANES, D_MODEL), np.float32)
    selk = np.zeros((LANES, D_MODEL), np.float32)
    oneq = np.zeros((1, D_MODEL), np.float32)
    onek = np.zeros((1, D_MODEL), np.float32)
    for h in range(FOX_HEADS):
        for p in range(3):
            selq[p * FOX_HEADS + h, h * HEAD_DIM + p] = 1.0
            selk[p * FOX_HEADS + h, h * HEAD_DIM + 3 + p] = -1.0
            oneq[0, h * HEAD_DIM + 3 + p] = 1.0
            onek[0, h * HEAD_DIM + p] = 1.0
    return (jnp.asarray(selq, BF16), jnp.asarray(selk, BF16),
            jnp.asarray(oneq), jnp.asarray(onek))


def _fox_norm_decay(h, g, w_in_t, layer, b_f):
    b_pad = jnp.pad(b_f, (0, LANES - FOX_HEADS)).reshape(1, LANES)
    selq, selk, oneq, onek = _decay_selectors()
    steps = SEQ // DECAY_ROWS
    full = lambda shape: pl.BlockSpec(shape, lambda b, s: (0, 0))
    rows = pl.BlockSpec((DECAY_ROWS, D_MODEL), lambda b, s: (b * steps + s, 0))
    out = jax.ShapeDtypeStruct((TOKENS, D_MODEL), BF16)
    gate_block = 4 * D_MODEL // FOX_HEADS
    return pl.pallas_call(
        _fox_decay_kernel,
        grid=(BATCH, steps),
        in_specs=[rows, full((1, D_MODEL)),
                  pl.BlockSpec((None, FOX_HEADS, D_MODEL),
                               lambda b, s: (layer, gate_block, 0)),
                  full((1, LANES)),
                  full((LANES, D_MODEL)), full((LANES, D_MODEL)),
                  full((1, D_MODEL)), full((1, D_MODEL))],
        out_specs=[rows, rows, rows],
        out_shape=[out, out, out],
        scratch_shapes=[pltpu.VMEM((DECAY_ROWS, LANES), F32),
                        pltpu.VMEM((8, LANES), F32)],
        compiler_params=_params("arbitrary", "arbitrary"),
        name="fox_decay",
    )(h, g.reshape(1, D_MODEL), w_in_t, b_pad, selq, selk, oneq, onek)


_NT = (((1,), (1,)), ((), ()))


ONES_ROWS = 16


def _transpose_values(v_ref, lanes, vt_ref):
    d = vt_ref.shape[0] - ONES_ROWS
    for c in range(0, SEQ, 512):
        vt_ref[:d, c:c + 512] = v_ref[0, c:c + 512, lanes].astype(F32).T.astype(BF16)
    vt_ref[d:, :] = jnp.ones((ONES_ROWS, SEQ), BF16)


def _exp_columns(sr, pr, m, n_tiles):
    for j in range(n_tiles):
        rows = slice(j * TK, (j + 1) * TK)
        pr[rows, :] = jnp.exp2(sr[rows, :] - m).astype(BF16)


def _pv_normalized(vt, pr, keys):
    d = vt.shape[0] - ONES_ROWS
    acc = jnp.dot(vt[:, :keys], pr[:keys, :], preferred_element_type=F32)
    return acc[:d] * (1.0 / acc[d:d + 1])


def _staged(n_units, stage_scores, stage_softmax, stage_output):
    stages = ((stage_scores, 0), (stage_softmax, 2), (stage_output, 4))
    for t in range(n_units + 4):
        for stage, lag in stages:
            if 0 <= t - lag < n_units:
                stage(t - lag)


def _fox_attn_kernel(q_ref, cq_ref, k_ref, ck_ref, v_ref, g_ref, o_ref,
                     s_ref, p_ref, vt_ref):
    krow = lax.broadcasted_iota(jnp.int32, (TK, TQ), 0)
    qcol = lax.broadcasted_iota(jnp.int32, (TK, TQ), 1)
    causal = krow <= qcol
    head_lanes = [slice(hh * HEAD_DIM, (hh + 1) * HEAD_DIM) for hh in range(FOX_HPS)]
    for hh in range(FOX_HPS):
        _transpose_values(v_ref, head_lanes[hh], vt_ref.at[hh])
    units = [(hh, i) for i in range(NQ) for hh in range(FOX_HPS)]
    maxes = {}

    def stage_scores(u):
        hh, i = units[u]
        hl = head_lanes[hh]
        qs, keys = i * TQ, i * TQ + TK
        sr = s_ref.at[u % FOX_SLOTS]
        qa = jnp.concatenate([q_ref[0, qs:qs + TQ, hl], cq_ref[0, qs:qs + TQ, hl]], axis=-1)
        ka = jnp.concatenate([k_ref[0, :keys, hl], ck_ref[0, :keys, hl]], axis=-1)
        s = lax.dot_general(ka, qa, _NT, preferred_element_type=F32)
        diag = jnp.where(causal, s[qs:keys], NEG)
        m = jnp.max(diag, axis=0, keepdims=True)
        if i > 0:
            sr[:qs, :] = s[:qs]
            m = jnp.maximum(m, jnp.max(s[:qs], axis=0, keepdims=True))
        sr[qs:keys, :] = diag
        maxes[u] = m

    def stage_softmax(u):
        _, i = units[u]
        _exp_columns(s_ref.at[u % FOX_SLOTS], p_ref.at[u % FOX_SLOTS], maxes.pop(u), i + 1)

    def stage_output(u):
        hh, i = units[u]
        hl = head_lanes[hh]
        qs, keys = i * TQ, i * TQ + TK
        out = _pv_normalized(vt_ref.at[hh], p_ref.at[u % FOX_SLOTS], keys).T
        gate = g_ref[0, qs:qs + TQ, hl].astype(F32)
        o_ref[0, qs:qs + TQ, hl] = (out * gate).astype(BF16)

    _staged(len(units), stage_scores, stage_softmax, stage_output)


def _fox_attention(proj, cq, ck):
    width = FOX_HPS * HEAD_DIM
    groups = FOX_HEADS // FOX_HPS
    blk = lambda off: pl.BlockSpec((1, SEQ, width), lambda b, h: (b, 0, off * groups + h))
    return pl.pallas_call(
        _fox_attn_kernel,
        grid=(BATCH, groups),
        in_specs=[blk(0), blk(0), blk(1), blk(0), blk(2), blk(3)],
        out_specs=blk(0),
        out_shape=jax.ShapeDtypeStruct((BATCH, SEQ, D_MODEL), BF16),
        scratch_shapes=[pltpu.VMEM((FOX_SLOTS, SEQ, TQ), F32),
                        pltpu.VMEM((FOX_SLOTS, SEQ, TQ), BF16),
                        pltpu.VMEM((FOX_HPS, HEAD_DIM + ONES_ROWS, SEQ), BF16)],
        compiler_params=_params("arbitrary", "arbitrary"),
        name="fox_attn",
    )(proj, cq, proj, ck, proj, proj)


def _bias_tile_kernel(rb_ref, o_ref):
    mp = pl.program_id(0)
    krow = lax.broadcasted_iota(jnp.int32, (TK, TQ), 0)
    qcol = lax.broadcasted_iota(jnp.int32, (TK, TQ), 1)
    max_exact = NUM_BUCKETS // 2
    for t in range(2):
        n = jnp.maximum(qcol - krow + t * TK, 0)
        nf = jnp.maximum(n, 1).astype(F32)
        large = max_exact + (jnp.log(nf / max_exact) / math.log(MAX_DISTANCE / max_exact)
                             * (NUM_BUCKETS - max_exact)).astype(jnp.int32)
        large = jnp.minimum(large, NUM_BUCKETS - 1)
        bucket = jnp.where(n < max_exact, n, large)
        val = jnp.zeros((TK, TQ), F32)
        for bkt in range(NUM_BUCKETS):
            val = jnp.where(bucket == bkt, rb_ref[bkt, mp], val)
        val = (val - rb_ref[NUM_BUCKETS - 1, mp]) * LOG2E
        if t == 0:
            val = jnp.where(krow <= qcol, val, NEG)
        o_ref[0, t] = val


def _bias_tiles(rel_bias):
    n_maps = 2 * DIFF_HEADS
    return pl.pallas_call(
        _bias_tile_kernel,
        grid=(n_maps,),
        in_specs=[pl.BlockSpec(memory_space=pltpu.SMEM)],
        out_specs=pl.BlockSpec((1, 2, TK, TQ), lambda mp: (mp, 0, 0, 0)),
        out_shape=jax.ShapeDtypeStruct((n_maps, 2, TK, TQ), F32),
        compiler_params=_params("arbitrary"),
        name="bias_tiles",
    )(rel_bias.reshape(NUM_BUCKETS, n_maps))


def _diff_attn_kernel(q1_ref, q2_ref, k1_ref, k2_ref, v_ref, bias_ref, lam_ref,
                      sub_ref, o_ref, s_ref, p_ref, vt_ref, *, lambda_init):
    lamf = lam_ref[...]
    lam_full = (jnp.exp(jnp.sum(lamf[0:1] * lamf[1:2], axis=-1, keepdims=True))
                - jnp.exp(jnp.sum(lamf[2:3] * lamf[3:4], axis=-1, keepdims=True))
                + lambda_init)
    _transpose_values(v_ref, slice(None), vt_ref)
    maps = ((q1_ref, k1_ref), (q2_ref, k2_ref))
    maxes = {}

    def slot(i, mp):
        return 2 * (i % (DIFF_SLOTS // 2)) + mp

    def stage_scores(i):
        qs, keys = i * TQ, i * TQ + TK
        near = qs - TK
        for mp, (q_ref, k_ref) in enumerate(maps):
            sr = s_ref.at[slot(i, mp)]
            s = lax.dot_general(k_ref[0, :keys, :], q_ref[0, qs:qs + TQ, :], _NT,
                                preferred_element_type=F32)
            diag = s[qs:keys] + bias_ref[mp, 0]
            m = jnp.max(diag, axis=0, keepdims=True)
            sr[qs:keys, :] = diag
            if i > 0:
                prev = s[near:qs] + bias_ref[mp, 1]
                m = jnp.maximum(m, jnp.max(prev, axis=0, keepdims=True))
                sr[near:qs, :] = prev
            if i > 1:
                m = jnp.maximum(m, jnp.max(s[:near], axis=0, keepdims=True))
                sr[:near, :] = s[:near]
            maxes[(i, mp)] = m

    def stage_softmax(i):
        for mp in range(2):
            sl = slot(i, mp)
            _exp_columns(s_ref.at[sl], p_ref.at[sl], maxes.pop((i, mp)), i + 1)

    def stage_output(i):
        qs, keys = i * TQ, i * TQ + TK
        outs = [_pv_normalized(vt_ref, p_ref.at[slot(i, mp)], keys) for mp in range(2)]
        o = (outs[0] - lam_full * outs[1]).T
        ms = jnp.mean(o * o, axis=-1, keepdims=True)
        o = o * lax.rsqrt(ms + SUBLN_EPS) * sub_ref[...] * (1.0 - lambda_init)
        o_ref[0, qs:qs + TQ, :] = o.astype(BF16)

    _staged(NQ, stage_scores, stage_softmax, stage_output)


def _diff_attention(proj, bias_tiles, lam, subln, lambda_init):
    qk = lambda off, j: pl.BlockSpec((1, SEQ, HEAD_DIM),
                                     lambda h, b: (b, 0, off + 2 * h + j))
    n_maps = 2 * DIFF_HEADS
    width = 2 * HEAD_DIM
    return pl.pallas_call(
        functools.partial(_diff_attn_kernel, lambda_init=lambda_init),
        grid=(DIFF_HEADS, BATCH),
        in_specs=[qk(0, 0), qk(0, 1), qk(n_maps, 0), qk(n_maps, 1),
                  pl.BlockSpec((1, SEQ, width), lambda h, b: (b, 0, 2 * DIFF_HEADS + h)),
                  pl.BlockSpec((2, 2, TK, TQ), lambda h, b: (h, 0, 0, 0)),
                  pl.BlockSpec((4, HEAD_DIM), lambda h, b: (0, 0)),
                  pl.BlockSpec((1, width), lambda h, b: (0, 0))],
        out_specs=pl.BlockSpec((1, SEQ, width), lambda h, b: (b, 0, h)),
        out_shape=jax.ShapeDtypeStruct((BATCH, SEQ, D_MODEL), BF16),
        scratch_shapes=[pltpu.VMEM((DIFF_SLOTS, SEQ, TQ), F32),
                        pltpu.VMEM((DIFF_SLOTS, SEQ, TQ), BF16),
                        pltpu.VMEM((width + ONES_ROWS, SEQ), BF16)],
        compiler_params=_params("arbitrary", "arbitrary"),
        name="diff_attn",
    )(proj, proj, proj, proj, proj, bias_tiles, lam, subln.reshape(1, width))


def kernel(x, attn_norm, ffn_norm, final_norm, fox_w_in, fox_b_f, fox_qk_norm, fox_w_out,
           diff_w_in, diff_lambda, diff_subln, diff_w_out, rel_bias,
           ffn_w_in, ffn_conv_w, ffn_conv_b, ffn_w_out):
    h = x.reshape(TOKENS, D_MODEL)
    bias_tiles = _bias_tiles(rel_bias)
    fox_w_in_t = jnp.swapaxes(fox_w_in, 1, 2)
    for i in range(DEPTH):
        j = i // 2
        if i % 2 == 0:
            a, cq, ck = _fox_norm_decay(h, attn_norm[i], fox_w_in_t, j, fox_b_f[j])
            proj = _fox_proj(a, fox_w_in_t, j, fox_qk_norm[j])
            o = _fox_attention(proj.reshape(BATCH, SEQ, 4 * D_MODEL),
                               cq.reshape(BATCH, SEQ, D_MODEL),
                               ck.reshape(BATCH, SEQ, D_MODEL))
            h, a = _attn_out(o.reshape(TOKENS, D_MODEL), fox_w_out, j, h, ffn_norm[i])
        else:
            lambda_init = 0.8 - 0.6 * math.exp(-0.3 * i)
            proj = _diff_proj(h, attn_norm[i], diff_w_in, j)
            o = _diff_attention(proj.reshape(BATCH, SEQ, 3 * D_MODEL), bias_tiles,
                                diff_lambda[j], diff_subln[j], lambda_init)
            h, a = _attn_out(o.reshape(TOKENS, D_MODEL), diff_w_out, j, h, ffn_norm[i])
        y = _ffn_in(a, ffn_w_in, i, ffn_conv_w[i], ffn_conv_b[i])
        h = _residual_matmul(y, ffn_w_out, i, h, "ffn_out")
    return _rmsnorm(h, final_norm, F32).reshape(BATCH, SEQ, D_MODEL)
```

```python
import functools
import math

import jax
import jax.numpy as jnp
import numpy as np
from jax import lax
from jax.experimental import pallas as pl
from jax.experimental.pallas import tpu as pltpu

D_MODEL = 2048
BATCH = 4
SEQ = 2048
TOKENS = BATCH * SEQ
DEPTH = 4
HEAD_DIM = 128
FOX_HEADS = D_MODEL // HEAD_DIM
DIFF_HEADS = D_MODEL // (2 * HEAD_DIM)
D_FF = 5632
CONV_WIDTH = 3
NUM_BUCKETS = 32
MAX_DISTANCE = 128
NORM_EPS = 1e-6
SUBLN_EPS = 1e-5

LANES = 128
VMEM_LIMIT = 56 * 1024 * 1024
NEG = -1e30
LOG2E = math.log2(math.e)
Q_SCALE = HEAD_DIM ** -0.5 * LOG2E

TQ = 256
TK = 256
NQ = SEQ // TQ
FOX_HPS = 2
FOX_SLOTS = 5
DIFF_SLOTS = 10

SUB_ROWS = 256
MM_TILES = {
    "fox_proj": (1024, 1024),
    "diff_proj": (2048, 1024),
    "attn_out": (512, D_MODEL),
    "ffn_in": (2048, 512),
    "ffn_out": (512, 1024),
}

F32 = jnp.float32
BF16 = jnp.bfloat16


def _params(*sem):
    return pltpu.CompilerParams(dimension_semantics=sem, vmem_limit_bytes=VMEM_LIMIT)


def _rmsnorm_kernel(x_ref, g_ref, o_ref):
    x = x_ref[...]
    ms = jnp.mean(x * x, axis=-1, keepdims=True)
    o_ref[...] = (x * lax.rsqrt(ms + NORM_EPS) * g_ref[...]).astype(o_ref.dtype)


def _rmsnorm(h, g, out_dtype):
    tm = 512
    return pl.pallas_call(
        _rmsnorm_kernel,
        grid=(TOKENS // tm,),
        in_specs=[pl.BlockSpec((tm, D_MODEL), lambda m: (m, 0)),
                  pl.BlockSpec((1, D_MODEL), lambda m: (0, 0))],
        out_specs=pl.BlockSpec((tm, D_MODEL), lambda m: (m, 0)),
        out_shape=jax.ShapeDtypeStruct((TOKENS, D_MODEL), out_dtype),
        compiler_params=_params("arbitrary"),
        name="rmsnorm",
    )(h, g.reshape(1, D_MODEL))


def _head_rmsnorm(acc, gain, mult):
    ms = jnp.mean(acc * acc, axis=-1, keepdims=True)
    return acc * lax.rsqrt(ms + NORM_EPS) * (gain * mult)


def _row_blocks(tm):
    return range(0, tm, SUB_ROWS)


def _fox_proj_kernel(x_ref, wt_ref, qkn_ref, o_ref, wb_ref, *, tm, tn):
    n = pl.program_id(0)

    @pl.when(pl.program_id(1) == 0)
    def _():
        for c in range(0, tn, SUB_ROWS):
            wb_ref[:, c:c + SUB_ROWS] = wt_ref[c:c + SUB_ROWS, :].T.astype(BF16)

    per = D_MODEL // tn

    def body(epilogue):
        for r in _row_blocks(tm):
            rows = slice(r, r + SUB_ROWS)
            acc = jnp.dot(x_ref[rows, :], wb_ref[...], preferred_element_type=F32)
            epilogue(acc, rows)

    def normed(row, mult):
        gain = qkn_ref[row:row + 1, :]

        def epilogue(acc, rows):
            for c in range(tn // HEAD_DIM):
                sl = slice(c * HEAD_DIM, (c + 1) * HEAD_DIM)
                o_ref[rows, sl] = _head_rmsnorm(acc[:, sl], gain, mult).astype(BF16)
        return epilogue

    def plain(acc, rows):
        o_ref[rows, :] = acc.astype(BF16)

    def gated(acc, rows):
        o_ref[rows, :] = jax.nn.sigmoid(acc).astype(BF16)

    @pl.when(n < per)
    def _():
        body(normed(0, Q_SCALE))

    @pl.when((n >= per) & (n < 2 * per))
    def _():
        body(normed(1, 1.0))

    @pl.when((n >= 2 * per) & (n < 3 * per))
    def _():
        body(plain)

    @pl.when(n >= 3 * per)
    def _():
        body(gated)


def _fox_proj(a, w_in_t, layer, qk_norm):
    tm, tn = MM_TILES["fox_proj"]
    n_out = 4 * D_MODEL
    return pl.pallas_call(
        functools.partial(_fox_proj_kernel, tm=tm, tn=tn),
        grid=(n_out // tn, TOKENS // tm),
        in_specs=[pl.BlockSpec((tm, D_MODEL), lambda n, m: (m, 0)),
                  pl.BlockSpec((None, tn, D_MODEL), lambda n, m: (layer, n, 0)),
                  pl.BlockSpec((2, HEAD_DIM), lambda n, m: (0, 0))],
        out_specs=pl.BlockSpec((tm, tn), lambda n, m: (m, n)),
        out_shape=jax.ShapeDtypeStruct((TOKENS, n_out), BF16),
        scratch_shapes=[pltpu.VMEM((D_MODEL, tn), BF16)],
        compiler_params=_params("arbitrary", "arbitrary"),
        name="fox_proj",
    )(a, w_in_t, qk_norm)


def _lane_tile(x, width):
    return jnp.concatenate([x] * (width // LANES), axis=1)


def _diff_proj_kernel(x_ref, ss_ref, g_ref, w_ref, o_ref, wb_ref, *, tm, tn):
    n = pl.program_id(0)

    @pl.when(pl.program_id(1) == 0)
    def _():
        wb_ref[...] = (w_ref[...] * _lane_tile(g_ref[...], tn)).astype(BF16)

    mult = jnp.where(n < D_MODEL // tn, Q_SCALE, 1.0).astype(F32)
    for r in _row_blocks(tm):
        rows = slice(r, r + SUB_ROWS)
        acc = jnp.dot(x_ref[rows, :], wb_ref[...], preferred_element_type=F32)
        ssq = ss_ref[0, rows, :]
        for t in range(1, ss_ref.shape[0]):
            ssq = ssq + ss_ref[t, rows, :]
        rs = lax.rsqrt(ssq * (1.0 / D_MODEL) + NORM_EPS) * mult
        o_ref[rows, :] = (acc * _lane_tile(rs, tn)).astype(BF16)


def _diff_proj(hb, ssq, g, w_in, layer):
    tm, tn = MM_TILES["diff_proj"]
    n_out = 3 * D_MODEL
    parts = ssq.shape[0]
    g_rows = jnp.broadcast_to(g.reshape(D_MODEL, 1), (D_MODEL, LANES))
    return pl.pallas_call(
        functools.partial(_diff_proj_kernel, tm=tm, tn=tn),
        grid=(n_out // tn, TOKENS // tm),
        in_specs=[pl.BlockSpec((tm, D_MODEL), lambda n, m: (m, 0)),
                  pl.BlockSpec((parts, tm, LANES), lambda n, m: (0, m, 0)),
                  pl.BlockSpec((D_MODEL, LANES), lambda n, m: (0, 0)),
                  pl.BlockSpec((None, D_MODEL, tn), lambda n, m: (layer, 0, n))],
        out_specs=pl.BlockSpec((tm, tn), lambda n, m: (m, n)),
        out_shape=jax.ShapeDtypeStruct((TOKENS, n_out), BF16),
        scratch_shapes=[pltpu.VMEM((D_MODEL, tn), BF16)],
        compiler_params=_params("arbitrary", "arbitrary"),
        name="diff_proj",
    )(hb, ssq, g_rows, w_in)


def _residual_matmul_kernel(x_ref, w_ref, h_ref, o_ref, *rest, tm, emit_stats):
    wb_ref = rest[-1]

    @pl.when(pl.program_id(1) == 0)
    def _():
        wb_ref[...] = w_ref[...].astype(BF16)

    for r in _row_blocks(tm):
        rows = slice(r, r + SUB_ROWS)
        acc = jnp.dot(x_ref[rows, :], wb_ref[...], preferred_element_type=F32)
        hn = h_ref[rows, :] + acc
        o_ref[rows, :] = hn
        if emit_stats:
            hb_ref, ss_ref = rest[:2]
            hb_ref[rows, :] = hn.astype(BF16)
            ssq = jnp.sum(hn * hn, axis=-1, keepdims=True)
            ss_ref[rows, :] = jnp.broadcast_to(ssq, (SUB_ROWS, LANES))


def _residual_matmul(x, w, layer, h, tiles, emit_stats=False):
    tm, tn = MM_TILES[tiles]
    k = x.shape[1]
    nt = D_MODEL // tn
    tile = pl.BlockSpec((tm, tn), lambda n, m: (m, n))
    out_specs = [tile]
    out_shape = [jax.ShapeDtypeStruct((TOKENS, D_MODEL), F32)]
    if emit_stats:
        out_specs += [tile, pl.BlockSpec((None, tm, LANES), lambda n, m: (n, m, 0))]
        out_shape += [jax.ShapeDtypeStruct((TOKENS, D_MODEL), BF16),
                      jax.ShapeDtypeStruct((nt, TOKENS, LANES), F32)]
    outs = pl.pallas_call(
        functools.partial(_residual_matmul_kernel, tm=tm, emit_stats=emit_stats),
        grid=(nt, TOKENS // tm),
        in_specs=[pl.BlockSpec((tm, k), lambda n, m: (m, 0)),
                  pl.BlockSpec((None, k, tn), lambda n, m: (layer, 0, n),
                               pipeline_mode=pl.Buffered(1)),
                  tile],
        out_specs=out_specs,
        out_shape=out_shape,
        scratch_shapes=[pltpu.VMEM((k, tn), BF16)],
        compiler_params=_params("arbitrary", "arbitrary"),
        name="residual_matmul",
    )(x, w, h)
    return outs if emit_stats else outs[0]


def _attn_out_kernel(x_ref, w_ref, h_ref, g_ref, o_ref, a_ref, wb_ref, *, tm):
    @pl.when(pl.program_id(0) == 0)
    def _():
        for c in range(0, D_MODEL, 512):
            wb_ref[c:c + 512, :] = w_ref[c:c + 512, :].astype(BF16)

    for r in _row_blocks(tm):
        rows = slice(r, r + SUB_ROWS)
        acc = jnp.dot(x_ref[rows, :], wb_ref[...], preferred_element_type=F32)
        hn = h_ref[rows, :] + acc
        o_ref[rows, :] = hn
        ms = jnp.mean(hn * hn, axis=-1, keepdims=True)
        a_ref[rows, :] = (hn * lax.rsqrt(ms + NORM_EPS) * g_ref[...]).astype(BF16)


def _attn_out(x, w, layer, h, g):
    tm, tn = MM_TILES["attn_out"]
    assert tn == D_MODEL
    rows = lambda m: (m, 0)
    return pl.pallas_call(
        functools.partial(_attn_out_kernel, tm=tm),
        grid=(TOKENS // tm,),
        in_specs=[pl.BlockSpec((tm, D_MODEL), rows),
                  pl.BlockSpec((None, D_MODEL, D_MODEL), lambda m: (layer, 0, 0),
                               pipeline_mode=pl.Buffered(1)),
                  pl.BlockSpec((tm, D_MODEL), rows),
                  pl.BlockSpec((1, D_MODEL), lambda m: (0, 0))],
        out_specs=[pl.BlockSpec((tm, D_MODEL), rows), pl.BlockSpec((tm, D_MODEL), rows)],
        out_shape=[jax.ShapeDtypeStruct((TOKENS, D_MODEL), F32),
                   jax.ShapeDtypeStruct((TOKENS, D_MODEL), BF16)],
        scratch_shapes=[pltpu.VMEM((D_MODEL, D_MODEL), BF16)],
        compiler_params=_params("arbitrary"),
        name="attn_out",
    )(x, w, h, g.reshape(1, D_MODEL))


CARRY_ROWS = 8


def _ffn_in_kernel(x_ref, wg_ref, wu_ref, cw_ref, cb_ref, o_ref,
                   wgb_ref, wub_ref, gs_ref, *, tm):
    m = pl.program_id(1)

    @pl.when(m == 0)
    def _():
        wgb_ref[...] = wg_ref[...].astype(BF16)
        wub_ref[...] = wu_ref[...].astype(BF16)

    @pl.when(m % (SEQ // tm) == 0)
    def _():
        gs_ref[0:CARRY_ROWS, :] = jnp.zeros((CARRY_ROWS, gs_ref.shape[1]), F32)

    for r in _row_blocks(tm):
        x = x_ref[r:r + SUB_ROWS, :]
        g = jnp.dot(x, wgb_ref[...], preferred_element_type=F32)
        u = jnp.dot(x, wub_ref[...], preferred_element_type=F32)
        base = CARRY_ROWS + r
        gs_ref[base:base + SUB_ROWS, :] = g
        g1 = gs_ref[base - 1:base - 1 + SUB_ROWS, :]
        g2 = gs_ref[base - 2:base - 2 + SUB_ROWS, :]
        conv = (cw_ref[2:3, :] * g + cw_ref[1:2, :] * g1 + cw_ref[0:1, :] * g2
                + cb_ref[...])
        gelu = 0.5 * conv * (1.0 + lax.erf(conv * (2.0 ** -0.5)))
        o_ref[r:r + SUB_ROWS, :] = (gelu * u).astype(BF16)
    gs_ref[0:CARRY_ROWS, :] = gs_ref[tm:tm + CARRY_ROWS, :]


def _ffn_in(a, w_in, layer, conv_w, conv_b):
    tm, tn = MM_TILES["ffn_in"]
    nt = D_FF // tn
    return pl.pallas_call(
        functools.partial(_ffn_in_kernel, tm=tm),
        grid=(nt, TOKENS // tm),
        in_specs=[pl.BlockSpec((tm, D_MODEL), lambda n, m: (m, 0)),
                  pl.BlockSpec((None, D_MODEL, tn), lambda n, m: (layer, 0, n)),
                  pl.BlockSpec((None, D_MODEL, tn), lambda n, m: (layer, 0, nt + n)),
                  pl.BlockSpec((CONV_WIDTH, tn), lambda n, m: (0, n)),
                  pl.BlockSpec((1, tn), lambda n, m: (0, n))],
        out_specs=pl.BlockSpec((tm, tn), lambda n, m: (m, n)),
        out_shape=jax.ShapeDtypeStruct((TOKENS, D_FF), BF16),
        scratch_shapes=[pltpu.VMEM((D_MODEL, tn), BF16),
                        pltpu.VMEM((D_MODEL, tn), BF16),
                        pltpu.VMEM((tm + CARRY_ROWS, tn), F32)],
        compiler_params=_params("arbitrary", "arbitrary"),
        name="ffn_in",
    )(a, w_in, w_in, conv_w, conv_b.reshape(1, D_FF))


def _split3(x):
    hi = x.astype(BF16)
    r1 = x - hi.astype(F32)
    mid = r1.astype(BF16)
    lo = (r1 - mid.astype(F32)).astype(BF16)
    return jnp.concatenate([hi, mid, lo], axis=-1)


def _split3_packed(c):
    hi = c.astype(BF16).astype(F32)
    r1 = c - hi
    mid = r1.astype(BF16).astype(F32)
    lo = r1 - mid
    lane = lax.broadcasted_iota(jnp.int32, c.shape, 1)
    packed = jnp.where(lane < FOX_HEADS, hi,
                       jnp.where(lane < 2 * FOX_HEADS, pltpu.roll(mid, FOX_HEADS, 1),
                                 jnp.where(lane < 3 * FOX_HEADS,
                                           pltpu.roll(lo, 2 * FOX_HEADS, 1), 0.0)))
    return packed.astype(BF16)


CUM_BLOCK = 512


DECAY_ROWS = 1024


def _fox_decay_kernel(h_ref, g_ref, wt_ref, b_ref, selq_ref, selk_ref, oneq_ref, onek_ref,
                      a_ref, cq_ref, ck_ref, c_ref, carry_ref):
    @pl.when(pl.program_id(1) == 0)
    def _():
        carry_ref[...] = jnp.zeros(carry_ref.shape, F32)

    wt = jnp.concatenate([wt_ref[...], jnp.zeros((LANES - FOX_HEADS, D_MODEL), F32)],
                         axis=0).astype(BF16)
    row = lax.broadcasted_iota(jnp.int32, (CUM_BLOCK, CUM_BLOCK), 0)
    col = lax.broadcasted_iota(jnp.int32, (CUM_BLOCK, CUM_BLOCK), 1)
    tri = jnp.where(row >= col, 1.0, 0.0).astype(BF16)
    carry = carry_ref[0:1, :]
    for blk in range(DECAY_ROWS // CUM_BLOCK):
        sl = slice(blk * CUM_BLOCK, (blk + 1) * CUM_BLOCK)
        x = h_ref[sl, :]
        ms = jnp.mean(x * x, axis=-1, keepdims=True)
        a = (x * lax.rsqrt(ms + NORM_EPS) * g_ref[...]).astype(BF16)
        a_ref[sl, :] = a
        z = lax.dot_general(a, wt, _NT, preferred_element_type=F32) + b_ref[...]
        logf = (jnp.minimum(z, 0.0) - jnp.log1p(jnp.exp(-jnp.abs(z)))) * LOG2E
        cs = jnp.dot(tri, _split3(logf), preferred_element_type=F32)
        c_blk = (cs[:, :LANES] + cs[:, LANES:2 * LANES] + cs[:, 2 * LANES:]) + carry
        carry = c_blk[CUM_BLOCK - 1:CUM_BLOCK, :]
        c_ref[sl, :] = c_blk
    carry_ref[0:1, :] = carry
    parts = _split3_packed(c_ref[...])
    cq = jnp.dot(parts, selq_ref[...], preferred_element_type=F32) + oneq_ref[...]
    ck = jnp.dot(parts, selk_ref[...], preferred_element_type=F32) + onek_ref[...]
    cq_ref[...] = cq.astype(BF16)
    ck_ref[...] = ck.astype(BF16)


def _decay_selectors():
    selq = np.zeros((LANES, D_MODEL), np.float32)
    selk = np.zeros((LANES, D_MODEL), np.float32)
    oneq = np.zeros((1, D_MODEL), np.float32)
    onek = np.zeros((1, D_MODEL), np.float32)
    for h in range(FOX_HEADS):
        for p in range(3):
            selq[p * FOX_HEADS + h, h * HEAD_DIM + p] = 1.0
            selk[p * FOX_HEADS + h, h * HEAD_DIM + 3 + p] = -1.0
            oneq[0, h * HEAD_DIM + 3 + p] = 1.0
            onek[0, h * HEAD_DIM + p] = 1.0
    return (jnp.asarray(selq, BF16), jnp.asarray(selk, BF16),
            jnp.asarray(oneq), jnp.asarray(onek))


def _fox_norm_decay(h, g, w_in_t, layer, b_f):
    b_pad = jnp.pad(b_f, (0, LANES - FOX_HEADS)).reshape(1, LANES)
    selq, selk, oneq, onek = _decay_selectors()
    steps = SEQ // DECAY_ROWS
    full = lambda shape: pl.BlockSpec(shape, lambda b, s: (0, 0))
    rows = pl.BlockSpec((DECAY_ROWS, D_MODEL), lambda b, s: (b * steps + s, 0))
    out = jax.ShapeDtypeStruct((TOKENS, D_MODEL), BF16)
    gate_block = 4 * D_MODEL // FOX_HEADS
    return pl.pallas_call(
        _fox_decay_kernel,
        grid=(BATCH, steps),
        in_specs=[rows, full((1, D_MODEL)),
                  pl.BlockSpec((None, FOX_HEADS, D_MODEL),
                               lambda b, s: (layer, gate_block, 0)),
                  full((1, LANES)),
                  full((LANES, D_MODEL)), full((LANES, D_MODEL)),
                  full((1, D_MODEL)), full((1, D_MODEL))],
        out_specs=[rows, rows, rows],
        out_shape=[out, out, out],
        scratch_shapes=[pltpu.VMEM((DECAY_ROWS, LANES), F32),
                        pltpu.VMEM((8, LANES), F32)],
        compiler_params=_params("arbitrary", "arbitrary"),
        name="fox_decay",
    )(h, g.reshape(1, D_MODEL), w_in_t, b_pad, selq, selk, oneq, onek)


_NT = (((1,), (1,)), ((), ()))


def _transpose_values(v_ref, lanes, vt_ref):
    for c in range(0, SEQ, 512):
        vt_ref[:, c:c + 512] = v_ref[0, c:c + 512, lanes].astype(F32).T.astype(BF16)


def _softmax_columns(sr, pr, n_tiles):
    m = jnp.max(sr[:n_tiles * TK, :], axis=0, keepdims=True)
    l = jnp.zeros((1, TQ), F32)
    for j in range(n_tiles):
        rows = slice(j * TK, (j + 1) * TK)
        p = jnp.exp2(sr[rows, :] - m)
        l = l + jnp.sum(p, axis=0, keepdims=True)
        pr[rows, :] = p.astype(BF16)
    return l


def _staged(n_units, stage_scores, stage_softmax, stage_output):
    stages = ((stage_scores, 0), (stage_softmax, 2), (stage_output, 4))
    for t in range(n_units + 4):
        for stage, lag in stages:
            if 0 <= t - lag < n_units:
                stage(t - lag)


def _fox_attn_kernel(q_ref, cq_ref, k_ref, ck_ref, v_ref, g_ref, o_ref,
                     s_ref, p_ref, vt_ref):
    krow = lax.broadcasted_iota(jnp.int32, (TK, TQ), 0)
    qcol = lax.broadcasted_iota(jnp.int32, (TK, TQ), 1)
    causal = krow <= qcol
    head_lanes = [slice(hh * HEAD_DIM, (hh + 1) * HEAD_DIM) for hh in range(FOX_HPS)]
    for hh in range(FOX_HPS):
        _transpose_values(v_ref, head_lanes[hh], vt_ref.at[hh])
    units = [(hh, i) for i in range(NQ) for hh in range(FOX_HPS)]
    sums = {}

    def stage_scores(u):
        hh, i = units[u]
        hl = head_lanes[hh]
        qs, keys = i * TQ, i * TQ + TK
        sr = s_ref.at[u % FOX_SLOTS]
        qa = jnp.concatenate([q_ref[0, qs:qs + TQ, hl], cq_ref[0, qs:qs + TQ, hl]], axis=-1)
        ka = jnp.concatenate([k_ref[0, :keys, hl], ck_ref[0, :keys, hl]], axis=-1)
        s = lax.dot_general(ka, qa, _NT, preferred_element_type=F32)
        if i > 0:
            sr[:qs, :] = s[:qs]
        sr[qs:keys, :] = jnp.where(causal, s[qs:keys], NEG)

    def stage_softmax(u):
        _, i = units[u]
        sums[u] = _softmax_columns(s_ref.at[u % FOX_SLOTS], p_ref.at[u % FOX_SLOTS], i + 1)

    def stage_output(u):
        hh, i = units[u]
        hl = head_lanes[hh]
        qs, keys = i * TQ, i * TQ + TK
        pr = p_ref.at[u % FOX_SLOTS]
        acc_t = jnp.dot(vt_ref[hh, :, :keys], pr[:keys, :], preferred_element_type=F32)
        out = (acc_t * (1.0 / sums.pop(u))).T
        gate = g_ref[0, qs:qs + TQ, hl].astype(F32)
        o_ref[0, qs:qs + TQ, hl] = (out * gate).astype(BF16)

    _staged(len(units), stage_scores, stage_softmax, stage_output)


def _fox_attention(proj, cq, ck):
    width = FOX_HPS * HEAD_DIM
    groups = FOX_HEADS // FOX_HPS
    blk = lambda off: pl.BlockSpec((1, SEQ, width), lambda b, h: (b, 0, off * groups + h))
    return pl.pallas_call(
        _fox_attn_kernel,
        grid=(BATCH, groups),
        in_specs=[blk(0), blk(0), blk(1), blk(0), blk(2), blk(3)],
        out_specs=blk(0),
        out_shape=jax.ShapeDtypeStruct((BATCH, SEQ, D_MODEL), BF16),
        scratch_shapes=[pltpu.VMEM((FOX_SLOTS, SEQ, TQ), F32),
                        pltpu.VMEM((FOX_SLOTS, SEQ, TQ), BF16),
                        pltpu.VMEM((FOX_HPS, HEAD_DIM, SEQ), BF16)],
        compiler_params=_params("arbitrary", "arbitrary"),
        name="fox_attn",
    )(proj, cq, proj, ck, proj, proj)


def _bias_tile_kernel(rb_ref, o_ref):
    mp = pl.program_id(0)
    krow = lax.broadcasted_iota(jnp.int32, (TK, TQ), 0)
    qcol = lax.broadcasted_iota(jnp.int32, (TK, TQ), 1)
    max_exact = NUM_BUCKETS // 2
    for t in range(2):
        n = jnp.maximum(qcol - krow + t * TK, 0)
        nf = jnp.maximum(n, 1).astype(F32)
        large = max_exact + (jnp.log(nf / max_exact) / math.log(MAX_DISTANCE / max_exact)
                             * (NUM_BUCKETS - max_exact)).astype(jnp.int32)
        large = jnp.minimum(large, NUM_BUCKETS - 1)
        bucket = jnp.where(n < max_exact, n, large)
        val = jnp.zeros((TK, TQ), F32)
        for bkt in range(NUM_BUCKETS):
            val = jnp.where(bucket == bkt, rb_ref[bkt, mp], val)
        val = (val - rb_ref[NUM_BUCKETS - 1, mp]) * LOG2E
        if t == 0:
            val = jnp.where(krow <= qcol, val, NEG)
        o_ref[0, t] = val


def _bias_tiles(rel_bias):
    n_maps = 2 * DIFF_HEADS
    return pl.pallas_call(
        _bias_tile_kernel,
        grid=(n_maps,),
        in_specs=[pl.BlockSpec(memory_space=pltpu.SMEM)],
        out_specs=pl.BlockSpec((1, 2, TK, TQ), lambda mp: (mp, 0, 0, 0)),
        out_shape=jax.ShapeDtypeStruct((n_maps, 2, TK, TQ), F32),
        compiler_params=_params("arbitrary"),
        name="bias_tiles",
    )(rel_bias.reshape(NUM_BUCKETS, n_maps))


def _diff_attn_kernel(q1_ref, q2_ref, k1_ref, k2_ref, v_ref, bias_ref, lam_ref,
                      sub_ref, o_ref, s_ref, p_ref, vt_ref, *, lambda_init):
    lamf = lam_ref[...]
    lam_full = (jnp.exp(jnp.sum(lamf[0:1] * lamf[1:2], axis=-1, keepdims=True))
                - jnp.exp(jnp.sum(lamf[2:3] * lamf[3:4], axis=-1, keepdims=True))
                + lambda_init)
    _transpose_values(v_ref, slice(None), vt_ref)
    maps = ((q1_ref, k1_ref), (q2_ref, k2_ref))
    sums = {}

    def slot(i, mp):
        return 2 * (i % (DIFF_SLOTS // 2)) + mp

    def stage_scores(i):
        qs, keys = i * TQ, i * TQ + TK
        near = qs - TK
        for mp, (q_ref, k_ref) in enumerate(maps):
            sr = s_ref.at[slot(i, mp)]
            s = lax.dot_general(k_ref[0, :keys, :], q_ref[0, qs:qs + TQ, :], _NT,
                                preferred_element_type=F32)
            if i > 1:
                sr[:near, :] = s[:near]
            if i > 0:
                sr[near:qs, :] = s[near:qs] + bias_ref[mp, 1]
            sr[qs:keys, :] = s[qs:keys] + bias_ref[mp, 0]

    def stage_softmax(i):
        for mp in range(2):
            sl = slot(i, mp)
            sums[(i, mp)] = _softmax_columns(s_ref.at[sl], p_ref.at[sl], i + 1)

    def stage_output(i):
        qs, keys = i * TQ, i * TQ + TK
        outs = []
        for mp in range(2):
            pr = p_ref.at[slot(i, mp)]
            acc_t = jnp.dot(vt_ref[:, :keys], pr[:keys, :], preferred_element_type=F32)
            outs.append(acc_t * (1.0 / sums.pop((i, mp))))
        o = (outs[0] - lam_full * outs[1]).T
        ms = jnp.mean(o * o, axis=-1, keepdims=True)
        o = o * lax.rsqrt(ms + SUBLN_EPS) * sub_ref[...] * (1.0 - lambda_init)
        o_ref[0, qs:qs + TQ, :] = o.astype(BF16)

    _staged(NQ, stage_scores, stage_softmax, stage_output)


def _diff_attention(proj, bias_tiles, lam, subln, lambda_init):
    qk = lambda off, j: pl.BlockSpec((1, SEQ, HEAD_DIM),
                                     lambda h, b: (b, 0, off + 2 * h + j))
    n_maps = 2 * DIFF_HEADS
    width = 2 * HEAD_DIM
    return pl.pallas_call(
        functools.partial(_diff_attn_kernel, lambda_init=lambda_init),
        grid=(DIFF_HEADS, BATCH),
        in_specs=[qk(0, 0), qk(0, 1), qk(n_maps, 0), qk(n_maps, 1),
                  pl.BlockSpec((1, SEQ, width), lambda h, b: (b, 0, 2 * DIFF_HEADS + h)),
                  pl.BlockSpec((2, 2, TK, TQ), lambda h, b: (h, 0, 0, 0)),
                  pl.BlockSpec((4, HEAD_DIM), lambda h, b: (0, 0)),
                  pl.BlockSpec((1, width), lambda h, b: (0, 0))],
        out_specs=pl.BlockSpec((1, SEQ, width), lambda h, b: (b, 0, h)),
        out_shape=jax.ShapeDtypeStruct((BATCH, SEQ, D_MODEL), BF16),
        scratch_shapes=[pltpu.VMEM((DIFF_SLOTS, SEQ, TQ), F32),
                        pltpu.VMEM((DIFF_SLOTS, SEQ, TQ), BF16),
                        pltpu.VMEM((width, SEQ), BF16)],
        compiler_params=_params("arbitrary", "arbitrary"),
        name="diff_attn",
    )(proj, proj, proj, proj, proj, bias_tiles, lam, subln.reshape(1, width))


def kernel(x, attn_norm, ffn_norm, final_norm, fox_w_in, fox_b_f, fox_qk_norm, fox_w_out,
           diff_w_in, diff_lambda, diff_subln, diff_w_out, rel_bias,
           ffn_w_in, ffn_conv_w, ffn_conv_b, ffn_w_out):
    h = x.reshape(TOKENS, D_MODEL)
    bias_tiles = _bias_tiles(rel_bias)
    fox_w_in_t = jnp.swapaxes(fox_w_in, 1, 2)
    for i in range(DEPTH):
        j = i // 2
        if i % 2 == 0:
            a, cq, ck = _fox_norm_decay(h, attn_norm[i], fox_w_in_t, j, fox_b_f[j])
            proj = _fox_proj(a, fox_w_in_t, j, fox_qk_norm[j])
            o = _fox_attention(proj.reshape(BATCH, SEQ, 4 * D_MODEL),
                               cq.reshape(BATCH, SEQ, D_MODEL),
                               ck.reshape(BATCH, SEQ, D_MODEL))
            h, a = _attn_out(o.reshape(TOKENS, D_MODEL), fox_w_out, j, h, ffn_norm[i])
        else:
            lambda_init = 0.8 - 0.6 * math.exp(-0.3 * i)
            proj = _diff_proj(hb, ssq, attn_norm[i], diff_w_in, j)
            o = _diff_attention(proj.reshape(BATCH, SEQ, 3 * D_MODEL), bias_tiles,
                                diff_lambda[j], diff_subln[j], lambda_init)
            h, a = _attn_out(o.reshape(TOKENS, D_MODEL), diff_w_out, j, h, ffn_norm[i])
        y = _ffn_in(a, ffn_w_in, i, ffn_conv_w[i], ffn_conv_b[i])
        if i % 2 == 0:
            h, hb, ssq = _residual_matmul(y, ffn_w_out, i, h, "ffn_out", emit_stats=True)
        else:
            h = _residual_matmul(y, ffn_w_out, i, h, "ffn_out")
    return _rmsnorm(h, final_norm, F32).reshape(BATCH, SEQ, D_MODEL)
```

```python
import functools
import math

import jax
import jax.numpy as jnp
import numpy as np
from jax import lax
from jax.experimental import pallas as pl
from jax.experimental.pallas import tpu as pltpu

D_MODEL = 2048
BATCH = 4
SEQ = 2048
TOKENS = BATCH * SEQ
DEPTH = 4
HEAD_DIM = 128
FOX_HEADS = D_MODEL // HEAD_DIM
DIFF_HEADS = D_MODEL // (2 * HEAD_DIM)
D_FF = 5632
CONV_WIDTH = 3
NUM_BUCKETS = 32
MAX_DISTANCE = 128
NORM_EPS = 1e-6
SUBLN_EPS = 1e-5

LANES = 128
VMEM_LIMIT = 56 * 1024 * 1024
NEG = -1e30
LOG2E = math.log2(math.e)
Q_SCALE = HEAD_DIM ** -0.5 * LOG2E

TQ = 256
TK = 256
NQ = SEQ // TQ
FOX_HPS = 2
FOX_SLOTS = 5
DIFF_SLOTS = 10

SUB_ROWS = 256
MM_TILES = {
    "fox_proj": (2048, 1024),
    "diff_proj": (2048, 1024),
    "attn_out": (512, D_MODEL),
    "ffn_in": (2048, 512),
    "ffn_out": (512, 1024),
}

F32 = jnp.float32
BF16 = jnp.bfloat16


def _params(*sem):
    return pltpu.CompilerParams(dimension_semantics=sem, vmem_limit_bytes=VMEM_LIMIT)


def _rmsnorm_kernel(x_ref, g_ref, o_ref):
    x = x_ref[...]
    ms = jnp.mean(x * x, axis=-1, keepdims=True)
    o_ref[...] = (x * lax.rsqrt(ms + NORM_EPS) * g_ref[...]).astype(o_ref.dtype)


def _rmsnorm(h, g, out_dtype):
    tm = 512
    return pl.pallas_call(
        _rmsnorm_kernel,
        grid=(TOKENS // tm,),
        in_specs=[pl.BlockSpec((tm, D_MODEL), lambda m: (m, 0)),
                  pl.BlockSpec((1, D_MODEL), lambda m: (0, 0))],
        out_specs=pl.BlockSpec((tm, D_MODEL), lambda m: (m, 0)),
        out_shape=jax.ShapeDtypeStruct((TOKENS, D_MODEL), out_dtype),
        compiler_params=_params("arbitrary"),
        name="rmsnorm",
    )(h, g.reshape(1, D_MODEL))


def _head_rmsnorm(acc, gain, mult):
    ms = jnp.mean(acc * acc, axis=-1, keepdims=True)
    return acc * lax.rsqrt(ms + NORM_EPS) * (gain * mult)


def _row_blocks(tm):
    return range(0, tm, SUB_ROWS)


def _fox_proj_kernel(x_ref, wt_ref, qkn_ref, o_ref, wb_ref, *, tm, tn):
    n = pl.program_id(0)

    @pl.when(pl.program_id(1) == 0)
    def _():
        for c in range(0, tn, SUB_ROWS):
            wb_ref[:, c:c + SUB_ROWS] = wt_ref[c:c + SUB_ROWS, :].T.astype(BF16)

    per = D_MODEL // tn

    def body(epilogue):
        for r in _row_blocks(tm):
            rows = slice(r, r + SUB_ROWS)
            acc = jnp.dot(x_ref[rows, :], wb_ref[...], preferred_element_type=F32)
            epilogue(acc, rows)

    def normed(acc, rows):
        gain = qkn_ref[pl.ds(n // per, 1), :]
        mult = jnp.where(n < per, Q_SCALE, 1.0).astype(F32)
        for c in range(tn // HEAD_DIM):
            sl = slice(c * HEAD_DIM, (c + 1) * HEAD_DIM)
            o_ref[rows, sl] = _head_rmsnorm(acc[:, sl], gain, mult).astype(BF16)

    def value_or_gate(acc, rows):
        o_ref[rows, :] = jnp.where(n >= 3 * per, jax.nn.sigmoid(acc), acc).astype(BF16)

    @pl.when(n < 2 * per)
    def _():
        body(normed)

    @pl.when(n >= 2 * per)
    def _():
        body(value_or_gate)


def _fox_proj(a, w_in_t, layer, qk_norm):
    tm, tn = MM_TILES["fox_proj"]
    n_out = 4 * D_MODEL
    return pl.pallas_call(
        functools.partial(_fox_proj_kernel, tm=tm, tn=tn),
        grid=(n_out // tn, TOKENS // tm),
        in_specs=[pl.BlockSpec((tm, D_MODEL), lambda n, m: (m, 0)),
                  pl.BlockSpec((None, tn, D_MODEL), lambda n, m: (layer, n, 0)),
                  pl.BlockSpec((2, HEAD_DIM), lambda n, m: (0, 0))],
        out_specs=pl.BlockSpec((tm, tn), lambda n, m: (m, n)),
        out_shape=jax.ShapeDtypeStruct((TOKENS, n_out), BF16),
        scratch_shapes=[pltpu.VMEM((D_MODEL, tn), BF16)],
        compiler_params=_params("arbitrary", "arbitrary"),
        name="fox_proj",
    )(a, w_in_t, qk_norm)


def _lane_tile(x, width):
    return jnp.concatenate([x] * (width // LANES), axis=1)


def _diff_proj_kernel(x_ref, ss_ref, g_ref, w_ref, o_ref, wb_ref, *, tm, tn):
    n = pl.program_id(0)

    @pl.when(pl.program_id(1) == 0)
    def _():
        wb_ref[...] = (w_ref[...] * _lane_tile(g_ref[...], tn)).astype(BF16)

    mult = jnp.where(n < D_MODEL // tn, Q_SCALE, 1.0).astype(F32)
    for r in _row_blocks(tm):
        rows = slice(r, r + SUB_ROWS)
        acc = jnp.dot(x_ref[rows, :], wb_ref[...], preferred_element_type=F32)
        ssq = ss_ref[0, rows, :]
        for t in range(1, ss_ref.shape[0]):
            ssq = ssq + ss_ref[t, rows, :]
        rs = lax.rsqrt(ssq * (1.0 / D_MODEL) + NORM_EPS) * mult
        o_ref[rows, :] = (acc * _lane_tile(rs, tn)).astype(BF16)


def _diff_proj(hb, ssq, g, w_in, layer):
    tm, tn = MM_TILES["diff_proj"]
    n_out = 3 * D_MODEL
    parts = ssq.shape[0]
    g_rows = jnp.broadcast_to(g.reshape(D_MODEL, 1), (D_MODEL, LANES))
    return pl.pallas_call(
        functools.partial(_diff_proj_kernel, tm=tm, tn=tn),
        grid=(n_out // tn, TOKENS // tm),
        in_specs=[pl.BlockSpec((tm, D_MODEL), lambda n, m: (m, 0)),
                  pl.BlockSpec((parts, tm, LANES), lambda n, m: (0, m, 0)),
                  pl.BlockSpec((D_MODEL, LANES), lambda n, m: (0, 0)),
                  pl.BlockSpec((None, D_MODEL, tn), lambda n, m: (layer, 0, n))],
        out_specs=pl.BlockSpec((tm, tn), lambda n, m: (m, n)),
        out_shape=jax.ShapeDtypeStruct((TOKENS, n_out), BF16),
        scratch_shapes=[pltpu.VMEM((D_MODEL, tn), BF16)],
        compiler_params=_params("arbitrary", "arbitrary"),
        name="diff_proj",
    )(hb, ssq, g_rows, w_in)


def _residual_matmul_kernel(x_ref, w_ref, h_ref, o_ref, *rest, tm, emit_stats):
    wb_ref = rest[-1]

    @pl.when(pl.program_id(1) == 0)
    def _():
        wb_ref[...] = w_ref[...].astype(BF16)

    for r in _row_blocks(tm):
        rows = slice(r, r + SUB_ROWS)
        acc = jnp.dot(x_ref[rows, :], wb_ref[...], preferred_element_type=F32)
        hn = h_ref[rows, :] + acc
        o_ref[rows, :] = hn
        if emit_stats:
            hb_ref, ss_ref = rest[:2]
            hb_ref[rows, :] = hn.astype(BF16)
            ssq = jnp.sum(hn * hn, axis=-1, keepdims=True)
            ss_ref[rows, :] = jnp.broadcast_to(ssq, (SUB_ROWS, LANES))


def _residual_matmul(x, w, layer, h, tiles, emit_stats=False):
    tm, tn = MM_TILES[tiles]
    k = x.shape[1]
    nt = D_MODEL // tn
    tile = pl.BlockSpec((tm, tn), lambda n, m: (m, n))
    out_specs = [tile]
    out_shape = [jax.ShapeDtypeStruct((TOKENS, D_MODEL), F32)]
    if emit_stats:
        out_specs += [tile, pl.BlockSpec((None, tm, LANES), lambda n, m: (n, m, 0))]
        out_shape += [jax.ShapeDtypeStruct((TOKENS, D_MODEL), BF16),
                      jax.ShapeDtypeStruct((nt, TOKENS, LANES), F32)]
    outs = pl.pallas_call(
        functools.partial(_residual_matmul_kernel, tm=tm, emit_stats=emit_stats),
        grid=(nt, TOKENS // tm),
        in_specs=[pl.BlockSpec((tm, k), lambda n, m: (m, 0)),
                  pl.BlockSpec((None, k, tn), lambda n, m: (layer, 0, n),
                               pipeline_mode=pl.Buffered(1)),
                  tile],
        out_specs=out_specs,
        out_shape=out_shape,
        scratch_shapes=[pltpu.VMEM((k, tn), BF16)],
        compiler_params=_params("arbitrary", "arbitrary"),
        name="residual_matmul",
    )(x, w, h)
    return outs if emit_stats else outs[0]


def _attn_out_kernel(x_ref, w_ref, h_ref, g_ref, o_ref, a_ref, wb_ref, *, tm):
    @pl.when(pl.program_id(0) == 0)
    def _():
        for c in range(0, D_MODEL, 512):
            wb_ref[c:c + 512, :] = w_ref[c:c + 512, :].astype(BF16)

    for r in _row_blocks(tm):
        rows = slice(r, r + SUB_ROWS)
        acc = jnp.dot(x_ref[rows, :], wb_ref[...], preferred_element_type=F32)
        hn = h_ref[rows, :] + acc
        o_ref[rows, :] = hn
        ms = jnp.mean(hn * hn, axis=-1, keepdims=True)
        a_ref[rows, :] = (hn * lax.rsqrt(ms + NORM_EPS) * g_ref[...]).astype(BF16)


def _attn_out(x, w, layer, h, g):
    tm, tn = MM_TILES["attn_out"]
    assert tn == D_MODEL
    rows = lambda m: (m, 0)
    return pl.pallas_call(
        functools.partial(_attn_out_kernel, tm=tm),
        grid=(TOKENS // tm,),
        in_specs=[pl.BlockSpec((tm, D_MODEL), rows),
                  pl.BlockSpec((None, D_MODEL, D_MODEL), lambda m: (layer, 0, 0),
                               pipeline_mode=pl.Buffered(1)),
                  pl.BlockSpec((tm, D_MODEL), rows),
                  pl.BlockSpec((1, D_MODEL), lambda m: (0, 0))],
        out_specs=[pl.BlockSpec((tm, D_MODEL), rows), pl.BlockSpec((tm, D_MODEL), rows)],
        out_shape=[jax.ShapeDtypeStruct((TOKENS, D_MODEL), F32),
                   jax.ShapeDtypeStruct((TOKENS, D_MODEL), BF16)],
        scratch_shapes=[pltpu.VMEM((D_MODEL, D_MODEL), BF16)],
        compiler_params=_params("arbitrary"),
        name="attn_out",
    )(x, w, h, g.reshape(1, D_MODEL))


CARRY_ROWS = 8


def _ffn_in_kernel(x_ref, wg_ref, wu_ref, cw_ref, cb_ref, o_ref,
                   wgb_ref, wub_ref, gs_ref, *, tm):
    m = pl.program_id(1)

    @pl.when(m == 0)
    def _():
        wgb_ref[...] = wg_ref[...].astype(BF16)
        wub_ref[...] = wu_ref[...].astype(BF16)

    @pl.when(m % (SEQ // tm) == 0)
    def _():
        gs_ref[0:CARRY_ROWS, :] = jnp.zeros((CARRY_ROWS, gs_ref.shape[1]), F32)

    for r in _row_blocks(tm):
        x = x_ref[r:r + SUB_ROWS, :]
        g = jnp.dot(x, wgb_ref[...], preferred_element_type=F32)
        u = jnp.dot(x, wub_ref[...], preferred_element_type=F32)
        base = CARRY_ROWS + r
        gs_ref[base:base + SUB_ROWS, :] = g
        g1 = gs_ref[base - 1:base - 1 + SUB_ROWS, :]
        g2 = gs_ref[base - 2:base - 2 + SUB_ROWS, :]
        conv = (cw_ref[2:3, :] * g + cw_ref[1:2, :] * g1 + cw_ref[0:1, :] * g2
                + cb_ref[...])
        gelu = 0.5 * conv * (1.0 + lax.erf(conv * (2.0 ** -0.5)))
        o_ref[r:r + SUB_ROWS, :] = (gelu * u).astype(BF16)
    gs_ref[0:CARRY_ROWS, :] = gs_ref[tm:tm + CARRY_ROWS, :]


def _ffn_in(a, w_in, layer, conv_w, conv_b):
    tm, tn = MM_TILES["ffn_in"]
    nt = D_FF // tn
    return pl.pallas_call(
        functools.partial(_ffn_in_kernel, tm=tm),
        grid=(nt, TOKENS // tm),
        in_specs=[pl.BlockSpec((tm, D_MODEL), lambda n, m: (m, 0)),
                  pl.BlockSpec((None, D_MODEL, tn), lambda n, m: (layer, 0, n)),
                  pl.BlockSpec((None, D_MODEL, tn), lambda n, m: (layer, 0, nt + n)),
                  pl.BlockSpec((CONV_WIDTH, tn), lambda n, m: (0, n)),
                  pl.BlockSpec((1, tn), lambda n, m: (0, n))],
        out_specs=pl.BlockSpec((tm, tn), lambda n, m: (m, n)),
        out_shape=jax.ShapeDtypeStruct((TOKENS, D_FF), BF16),
        scratch_shapes=[pltpu.VMEM((D_MODEL, tn), BF16),
                        pltpu.VMEM((D_MODEL, tn), BF16),
                        pltpu.VMEM((tm + CARRY_ROWS, tn), F32)],
        compiler_params=_params("arbitrary", "arbitrary"),
        name="ffn_in",
    )(a, w_in, w_in, conv_w, conv_b.reshape(1, D_FF))


def _split3(x):
    hi = x.astype(BF16)
    r1 = x - hi.astype(F32)
    mid = r1.astype(BF16)
    lo = (r1 - mid.astype(F32)).astype(BF16)
    return jnp.concatenate([hi, mid, lo], axis=-1)


def _split3_packed(c):
    hi = c.astype(BF16).astype(F32)
    r1 = c - hi
    mid = r1.astype(BF16).astype(F32)
    lo = r1 - mid
    lane = lax.broadcasted_iota(jnp.int32, c.shape, 1)
    packed = jnp.where(lane < FOX_HEADS, hi,
                       jnp.where(lane < 2 * FOX_HEADS, pltpu.roll(mid, FOX_HEADS, 1),
                                 jnp.where(lane < 3 * FOX_HEADS,
                                           pltpu.roll(lo, 2 * FOX_HEADS, 1), 0.0)))
    return packed.astype(BF16)


CUM_BLOCK = 512


DECAY_ROWS = 1024


def _fox_decay_kernel(h_ref, g_ref, wt_ref, b_ref, selq_ref, selk_ref, oneq_ref, onek_ref,
                      a_ref, cq_ref, ck_ref, c_ref, carry_ref):
    @pl.when(pl.program_id(1) == 0)
    def _():
        carry_ref[...] = jnp.zeros(carry_ref.shape, F32)

    wt = jnp.concatenate([wt_ref[...], jnp.zeros((LANES - FOX_HEADS, D_MODEL), F32)],
                         axis=0).astype(BF16)
    row = lax.broadcasted_iota(jnp.int32, (CUM_BLOCK, CUM_BLOCK), 0)
    col = lax.broadcasted_iota(jnp.int32, (CUM_BLOCK, CUM_BLOCK), 1)
    tri = jnp.where(row >= col, 1.0, 0.0).astype(BF16)
    carry = carry_ref[0:1, :]
    for blk in range(DECAY_ROWS // CUM_BLOCK):
        sl = slice(blk * CUM_BLOCK, (blk + 1) * CUM_BLOCK)
        x = h_ref[sl, :]
        ms = jnp.mean(x * x, axis=-1, keepdims=True)
        a = (x * lax.rsqrt(ms + NORM_EPS) * g_ref[...]).astype(BF16)
        a_ref[sl, :] = a
        z = lax.dot_general(a, wt, _NT, preferred_element_type=F32) + b_ref[...]
        logf = (jnp.minimum(z, 0.0) - jnp.log1p(jnp.exp(-jnp.abs(z)))) * LOG2E
        cs = jnp.dot(tri, _split3(logf), preferred_element_type=F32)
        c_blk = (cs[:, :LANES] + cs[:, LANES:2 * LANES] + cs[:, 2 * LANES:]) + carry
        carry = c_blk[CUM_BLOCK - 1:CUM_BLOCK, :]
        c_ref[sl, :] = c_blk
    carry_ref[0:1, :] = carry
    parts = _split3_packed(c_ref[...])
    cq = jnp.dot(parts, selq_ref[...], preferred_element_type=F32) + oneq_ref[...]
    ck = jnp.dot(parts, selk_ref[...], preferred_element_type=F32) + onek_ref[...]
    cq_ref[...] = cq.astype(BF16)
    ck_ref[...] = ck.astype(BF16)


def _decay_selectors():
    selq = np.zeros((LANES, D_MODEL), np.float32)
    selk = np.zeros((LANES, D_MODEL), np.float32)
    oneq = np.zeros((1, D_MODEL), np.float32)
    onek = np.zeros((1, D_MODEL), np.float32)
    for h in range(FOX_HEADS):
        for p in range(3):
            selq[p * FOX_HEADS + h, h * HEAD_DIM + p] = 1.0
            selk[p * FOX_HEADS + h, h * HEAD_DIM + 3 + p] = -1.0
            oneq[0, h * HEAD_DIM + 3 + p] = 1.0
            onek[0, h * HEAD_DIM + p] = 1.0
    return (jnp.asarray(selq, BF16), jnp.asarray(selk, BF16),
            jnp.asarray(oneq), jnp.asarray(onek))


def _fox_norm_decay(h, g, w_in_t, layer, b_f):
    b_pad = jnp.pad(b_f, (0, LANES - FOX_HEADS)).reshape(1, LANES)
    selq, selk, oneq, onek = _decay_selectors()
    steps = SEQ // DECAY_ROWS
    full = lambda shape: pl.BlockSpec(shape, lambda b, s: (0, 0))
    rows = pl.BlockSpec((DECAY_ROWS, D_MODEL), lambda b, s: (b * steps + s, 0))
    out = jax.ShapeDtypeStruct((TOKENS, D_MODEL), BF16)
    gate_block = 4 * D_MODEL // FOX_HEADS
    return pl.pallas_call(
        _fox_decay_kernel,
        grid=(BATCH, steps),
        in_specs=[rows, full((1, D_MODEL)),
                  pl.BlockSpec((None, FOX_HEADS, D_MODEL),
                               lambda b, s: (layer, gate_block, 0)),
                  full((1, LANES)),
                  full((LANES, D_MODEL)), full((LANES, D_MODEL)),
                  full((1, D_MODEL)), full((1, D_MODEL))],
        out_specs=[rows, rows, rows],
        out_shape=[out, out, out],
        scratch_shapes=[pltpu.VMEM((DECAY_ROWS, LANES), F32),
                        pltpu.VMEM((8, LANES), F32)],
        compiler_params=_params("arbitrary", "arbitrary"),
        name="fox_decay",
    )(h, g.reshape(1, D_MODEL), w_in_t, b_pad, selq, selk, oneq, onek)


_NT = (((1,), (1,)), ((), ()))


def _transpose_values(v_ref, lanes, vt_ref):
    for c in range(0, SEQ, 512):
        vt_ref[:, c:c + 512] = v_ref[0, c:c + 512, lanes].astype(F32).T.astype(BF16)


def _softmax_columns(sr, pr, n_tiles):
    m = jnp.max(sr[:n_tiles * TK, :], axis=0, keepdims=True)
    l = jnp.zeros((1, TQ), F32)
    for j in range(n_tiles):
        rows = slice(j * TK, (j + 1) * TK)
        p = jnp.exp2(sr[rows, :] - m)
        l = l + jnp.sum(p, axis=0, keepdims=True)
        pr[rows, :] = p.astype(BF16)
    return l


def _staged(n_units, stage_scores, stage_softmax, stage_output):
    stages = ((stage_scores, 0), (stage_softmax, 2), (stage_output, 4))
    for t in range(n_units + 4):
        for stage, lag in stages:
            if 0 <= t - lag < n_units:
                stage(t - lag)


def _fox_attn_kernel(q_ref, cq_ref, k_ref, ck_ref, v_ref, g_ref, o_ref,
                     s_ref, p_ref, vt_ref):
    krow = lax.broadcasted_iota(jnp.int32, (TK, TQ), 0)
    qcol = lax.broadcasted_iota(jnp.int32, (TK, TQ), 1)
    causal = krow <= qcol
    head_lanes = [slice(hh * HEAD_DIM, (hh + 1) * HEAD_DIM) for hh in range(FOX_HPS)]
    for hh in range(FOX_HPS):
        _transpose_values(v_ref, head_lanes[hh], vt_ref.at[hh])
    units = [(hh, i) for i in range(NQ) for hh in range(FOX_HPS)]
    sums = {}

    def stage_scores(u):
        hh, i = units[u]
        hl = head_lanes[hh]
        qs, keys = i * TQ, i * TQ + TK
        sr = s_ref.at[u % FOX_SLOTS]
        qa = jnp.concatenate([q_ref[0, qs:qs + TQ, hl], cq_ref[0, qs:qs + TQ, hl]], axis=-1)
        ka = jnp.concatenate([k_ref[0, :keys, hl], ck_ref[0, :keys, hl]], axis=-1)
        s = lax.dot_general(ka, qa, _NT, preferred_element_type=F32)
        if i > 0:
            sr[:qs, :] = s[:qs]
        sr[qs:keys, :] = jnp.where(causal, s[qs:keys], NEG)

    def stage_softmax(u):
        _, i = units[u]
        sums[u] = _softmax_columns(s_ref.at[u % FOX_SLOTS], p_ref.at[u % FOX_SLOTS], i + 1)

    def stage_output(u):
        hh, i = units[u]
        hl = head_lanes[hh]
        qs, keys = i * TQ, i * TQ + TK
        pr = p_ref.at[u % FOX_SLOTS]
        acc_t = jnp.dot(vt_ref[hh, :, :keys], pr[:keys, :], preferred_element_type=F32)
        out = (acc_t * (1.0 / sums.pop(u))).T
        gate = g_ref[0, qs:qs + TQ, hl].astype(F32)
        o_ref[0, qs:qs + TQ, hl] = (out * gate).astype(BF16)

    _staged(len(units), stage_scores, stage_softmax, stage_output)


def _fox_attention(proj, cq, ck):
    width = FOX_HPS * HEAD_DIM
    groups = FOX_HEADS // FOX_HPS
    blk = lambda off: pl.BlockSpec((1, SEQ, width), lambda b, h: (b, 0, off * groups + h))
    return pl.pallas_call(
        _fox_attn_kernel,
        grid=(BATCH, groups),
        in_specs=[blk(0), blk(0), blk(1), blk(0), blk(2), blk(3)],
        out_specs=blk(0),
        out_shape=jax.ShapeDtypeStruct((BATCH, SEQ, D_MODEL), BF16),
        scratch_shapes=[pltpu.VMEM((FOX_SLOTS, SEQ, TQ), F32),
                        pltpu.VMEM((FOX_SLOTS, SEQ, TQ), BF16),
                        pltpu.VMEM((FOX_HPS, HEAD_DIM, SEQ), BF16)],
        compiler_params=_params("arbitrary", "arbitrary"),
        name="fox_attn",
    )(proj, cq, proj, ck, proj, proj)


def _bias_tile_kernel(rb_ref, o_ref):
    mp = pl.program_id(0)
    krow = lax.broadcasted_iota(jnp.int32, (TK, TQ), 0)
    qcol = lax.broadcasted_iota(jnp.int32, (TK, TQ), 1)
    max_exact = NUM_BUCKETS // 2
    for t in range(2):
        n = jnp.maximum(qcol - krow + t * TK, 0)
        nf = jnp.maximum(n, 1).astype(F32)
        large = max_exact + (jnp.log(nf / max_exact) / math.log(MAX_DISTANCE / max_exact)
                             * (NUM_BUCKETS - max_exact)).astype(jnp.int32)
        large = jnp.minimum(large, NUM_BUCKETS - 1)
        bucket = jnp.where(n < max_exact, n, large)
        val = jnp.zeros((TK, TQ), F32)
        for bkt in range(NUM_BUCKETS):
            val = jnp.where(bucket == bkt, rb_ref[bkt, mp], val)
        val = (val - rb_ref[NUM_BUCKETS - 1, mp]) * LOG2E
        if t == 0:
            val = jnp.where(krow <= qcol, val, NEG)
        o_ref[0, t] = val


def _bias_tiles(rel_bias):
    n_maps = 2 * DIFF_HEADS
    return pl.pallas_call(
        _bias_tile_kernel,
        grid=(n_maps,),
        in_specs=[pl.BlockSpec(memory_space=pltpu.SMEM)],
        out_specs=pl.BlockSpec((1, 2, TK, TQ), lambda mp: (mp, 0, 0, 0)),
        out_shape=jax.ShapeDtypeStruct((n_maps, 2, TK, TQ), F32),
        compiler_params=_params("arbitrary"),
        name="bias_tiles",
    )(rel_bias.reshape(NUM_BUCKETS, n_maps))


def _diff_attn_kernel(q1_ref, q2_ref, k1_ref, k2_ref, v_ref, bias_ref, lam_ref,
                      sub_ref, o_ref, s_ref, p_ref, vt_ref, *, lambda_init):
    lamf = lam_ref[...]
    lam_full = (jnp.exp(jnp.sum(lamf[0:1] * lamf[1:2], axis=-1, keepdims=True))
                - jnp.exp(jnp.sum(lamf[2:3] * lamf[3:4], axis=-1, keepdims=True))
                + lambda_init)
    _transpose_values(v_ref, slice(None), vt_ref)
    maps = ((q1_ref, k1_ref), (q2_ref, k2_ref))
    sums = {}

    def slot(i, mp):
        return 2 * (i % (DIFF_SLOTS // 2)) + mp

    def stage_scores(i):
        qs, keys = i * TQ, i * TQ + TK
        near = qs - TK
        for mp, (q_ref, k_ref) in enumerate(maps):
            sr = s_ref.at[slot(i, mp)]
            s = lax.dot_general(k_ref[0, :keys, :], q_ref[0, qs:qs + TQ, :], _NT,
                                preferred_element_type=F32)
            if i > 1:
                sr[:near, :] = s[:near]
            if i > 0:
                sr[near:qs, :] = s[near:qs] + bias_ref[mp, 1]
            sr[qs:keys, :] = s[qs:keys] + bias_ref[mp, 0]

    def stage_softmax(i):
        for mp in range(2):
            sl = slot(i, mp)
            sums[(i, mp)] = _softmax_columns(s_ref.at[sl], p_ref.at[sl], i + 1)

    def stage_output(i):
        qs, keys = i * TQ, i * TQ + TK
        outs = []
        for mp in range(2):
            pr = p_ref.at[slot(i, mp)]
            acc_t = jnp.dot(vt_ref[:, :keys], pr[:keys, :], preferred_element_type=F32)
            outs.append(acc_t * (1.0 / sums.pop((i, mp))))
        o = (outs[0] - lam_full * outs[1]).T
        ms = jnp.mean(o * o, axis=-1, keepdims=True)
        o = o * lax.rsqrt(ms + SUBLN_EPS) * sub_ref[...] * (1.0 - lambda_init)
        o_ref[0, qs:qs + TQ, :] = o.astype(BF16)

    _staged(NQ, stage_scores, stage_softmax, stage_output)


def _diff_attention(proj, bias_tiles, lam, subln, lambda_init):
    qk = lambda off, j: pl.BlockSpec((1, SEQ, HEAD_DIM),
                                     lambda h, b: (b, 0, off + 2 * h + j))
    n_maps = 2 * DIFF_HEADS
    width = 2 * HEAD_DIM
    return pl.pallas_call(
        functools.partial(_diff_attn_kernel, lambda_init=lambda_init),
        grid=(DIFF_HEADS, BATCH),
        in_specs=[qk(0, 0), qk(0, 1), qk(n_maps, 0), qk(n_maps, 1),
                  pl.BlockSpec((1, SEQ, width), lambda h, b: (b, 0, 2 * DIFF_HEADS + h)),
                  pl.BlockSpec((2, 2, TK, TQ), lambda h, b: (h, 0, 0, 0)),
                  pl.BlockSpec((4, HEAD_DIM), lambda h, b: (0, 0)),
                  pl.BlockSpec((1, width), lambda h, b: (0, 0))],
        out_specs=pl.BlockSpec((1, SEQ, width), lambda h, b: (b, 0, h)),
        out_shape=jax.ShapeDtypeStruct((BATCH, SEQ, D_MODEL), BF16),
        scratch_shapes=[pltpu.VMEM((DIFF_SLOTS, SEQ, TQ), F32),
                        pltpu.VMEM((DIFF_SLOTS, SEQ, TQ), BF16),
                        pltpu.VMEM((width, SEQ), BF16)],
        compiler_params=_params("arbitrary", "arbitrary"),
        name="diff_attn",
    )(proj, proj, proj, proj, proj, bias_tiles, lam, subln.reshape(1, width))


def kernel(x, attn_norm, ffn_norm, final_norm, fox_w_in, fox_b_f, fox_qk_norm, fox_w_out,
           diff_w_in, diff_lambda, diff_subln, diff_w_out, rel_bias,
           ffn_w_in, ffn_conv_w, ffn_conv_b, ffn_w_out):
    h = x.reshape(TOKENS, D_MODEL)
    bias_tiles = _bias_tiles(rel_bias)
    fox_w_in_t = jnp.swapaxes(fox_w_in, 1, 2)
    for i in range(DEPTH):
        j = i // 2
        if i % 2 == 0:
            a, cq, ck = _fox_norm_decay(h, attn_norm[i], fox_w_in_t, j, fox_b_f[j])
            proj = _fox_proj(a, fox_w_in_t, j, fox_qk_norm[j])
            o = _fox_attention(proj.reshape(BATCH, SEQ, 4 * D_MODEL),
                               cq.reshape(BATCH, SEQ, D_MODEL),
                               ck.reshape(BATCH, SEQ, D_MODEL))
            h, a = _attn_out(o.reshape(TOKENS, D_MODEL), fox_w_out, j, h, ffn_norm[i])
        else:
            lambda_init = 0.8 - 0.6 * math.exp(-0.3 * i)
            proj = _diff_proj(hb, ssq, attn_norm[i], diff_w_in, j)
            o = _diff_attention(proj.reshape(BATCH, SEQ, 3 * D_MODEL), bias_tiles,
                                diff_lambda[j], diff_subln[j], lambda_init)
            h, a = _attn_out(o.reshape(TOKENS, D_MODEL), diff_w_out, j, h, ffn_norm[i])
        y = _ffn_in(a, ffn_w_in, i, ffn_conv_w[i], ffn_conv_b[i])
        if i % 2 == 0:
            h, hb, ssq = _residual_matmul(y, ffn_w_out, i, h, "ffn_out", emit_stats=True)
        else:
            h = _residual_matmul(y, ffn_w_out, i, h, "ffn_out")
    return _rmsnorm(h, final_norm, F32).reshape(BATCH, SEQ, D_MODEL)
```

```python
import functools
import math

import jax
import jax.numpy as jnp
import numpy as np
from jax import lax
from jax.experimental import pallas as pl
from jax.experimental.pallas import tpu as pltpu

D_MODEL = 2048
BATCH = 4
SEQ = 2048
TOKENS = BATCH * SEQ
DEPTH = 4
HEAD_DIM = 128
FOX_HEADS = D_MODEL // HEAD_DIM
DIFF_HEADS = D_MODEL // (2 * HEAD_DIM)
D_FF = 5632
CONV_WIDTH = 3
NUM_BUCKETS = 32
MAX_DISTANCE = 128
NORM_EPS = 1e-6
SUBLN_EPS = 1e-5

LANES = 128
VMEM_LIMIT = 56 * 1024 * 1024
NEG = -1e30
LOG2E = math.log2(math.e)
Q_SCALE = HEAD_DIM ** -0.5 * LOG2E

TQ = 256
TK = 256
NQ = SEQ // TQ
FOX_HPS = 2
FOX_SLOTS = 5
DIFF_SLOTS = 10

SUB_ROWS = 256
MM_TILES = {
    "fox_proj": (2048, 1024),
    "diff_proj": (2048, 1024),
    "attn_out": (1024, 1024),
    "ffn_in": (2048, 512),
    "ffn_out": (512, 1024),
}

F32 = jnp.float32
BF16 = jnp.bfloat16


def _params(*sem):
    return pltpu.CompilerParams(dimension_semantics=sem, vmem_limit_bytes=VMEM_LIMIT)


def _rmsnorm_kernel(x_ref, g_ref, o_ref):
    x = x_ref[...]
    ms = jnp.mean(x * x, axis=-1, keepdims=True)
    o_ref[...] = (x * lax.rsqrt(ms + NORM_EPS) * g_ref[...]).astype(o_ref.dtype)


def _rmsnorm(h, g, out_dtype):
    tm = 512
    return pl.pallas_call(
        _rmsnorm_kernel,
        grid=(TOKENS // tm,),
        in_specs=[pl.BlockSpec((tm, D_MODEL), lambda m: (m, 0)),
                  pl.BlockSpec((1, D_MODEL), lambda m: (0, 0))],
        out_specs=pl.BlockSpec((tm, D_MODEL), lambda m: (m, 0)),
        out_shape=jax.ShapeDtypeStruct((TOKENS, D_MODEL), out_dtype),
        compiler_params=_params("arbitrary"),
        name="rmsnorm",
    )(h, g.reshape(1, D_MODEL))


def _head_rmsnorm(acc, gain, mult):
    ms = jnp.mean(acc * acc, axis=-1, keepdims=True)
    return acc * lax.rsqrt(ms + NORM_EPS) * (gain * mult)


def _row_blocks(tm):
    return range(0, tm, SUB_ROWS)


def _fox_proj_kernel(x_ref, wt_ref, qkn_ref, o_ref, wb_ref, *, tm, tn):
    n = pl.program_id(0)

    @pl.when(pl.program_id(1) == 0)
    def _():
        for c in range(0, tn, SUB_ROWS):
            wb_ref[:, c:c + SUB_ROWS] = wt_ref[c:c + SUB_ROWS, :].T.astype(BF16)

    per = D_MODEL // tn

    def body(epilogue):
        for r in _row_blocks(tm):
            rows = slice(r, r + SUB_ROWS)
            acc = jnp.dot(x_ref[rows, :], wb_ref[...], preferred_element_type=F32)
            epilogue(acc, rows)

    def normed(acc, rows):
        gain = qkn_ref[pl.ds(n // per, 1), :]
        mult = jnp.where(n < per, Q_SCALE, 1.0).astype(F32)
        for c in range(tn // HEAD_DIM):
            sl = slice(c * HEAD_DIM, (c + 1) * HEAD_DIM)
            o_ref[rows, sl] = _head_rmsnorm(acc[:, sl], gain, mult).astype(BF16)

    def value_or_gate(acc, rows):
        o_ref[rows, :] = jnp.where(n >= 3 * per, jax.nn.sigmoid(acc), acc).astype(BF16)

    @pl.when(n < 2 * per)
    def _():
        body(normed)

    @pl.when(n >= 2 * per)
    def _():
        body(value_or_gate)


def _fox_proj(a, w_in_t, layer, qk_norm):
    tm, tn = MM_TILES["fox_proj"]
    n_out = 4 * D_MODEL
    return pl.pallas_call(
        functools.partial(_fox_proj_kernel, tm=tm, tn=tn),
        grid=(n_out // tn, TOKENS // tm),
        in_specs=[pl.BlockSpec((tm, D_MODEL), lambda n, m: (m, 0)),
                  pl.BlockSpec((None, tn, D_MODEL), lambda n, m: (layer, n, 0)),
                  pl.BlockSpec((2, HEAD_DIM), lambda n, m: (0, 0))],
        out_specs=pl.BlockSpec((tm, tn), lambda n, m: (m, n)),
        out_shape=jax.ShapeDtypeStruct((TOKENS, n_out), BF16),
        scratch_shapes=[pltpu.VMEM((D_MODEL, tn), BF16)],
        compiler_params=_params("arbitrary", "arbitrary"),
        name="fox_proj",
    )(a, w_in_t, qk_norm)


def _lane_tile(x, width):
    return jnp.concatenate([x] * (width // LANES), axis=1)


def _diff_proj_kernel(x_ref, ss_ref, g_ref, w_ref, o_ref, wb_ref, *, tm, tn):
    n = pl.program_id(0)

    @pl.when(pl.program_id(1) == 0)
    def _():
        wb_ref[...] = (w_ref[...] * _lane_tile(g_ref[...], tn)).astype(BF16)

    mult = jnp.where(n < D_MODEL // tn, Q_SCALE, 1.0).astype(F32)
    for r in _row_blocks(tm):
        rows = slice(r, r + SUB_ROWS)
        acc = jnp.dot(x_ref[rows, :], wb_ref[...], preferred_element_type=F32)
        ssq = ss_ref[0, rows, :]
        for t in range(1, ss_ref.shape[0]):
            ssq = ssq + ss_ref[t, rows, :]
        rs = lax.rsqrt(ssq * (1.0 / D_MODEL) + NORM_EPS) * mult
        o_ref[rows, :] = (acc * _lane_tile(rs, tn)).astype(BF16)


def _diff_proj(hb, ssq, g, w_in, layer):
    tm, tn = MM_TILES["diff_proj"]
    n_out = 3 * D_MODEL
    parts = ssq.shape[0]
    g_rows = jnp.broadcast_to(g.reshape(D_MODEL, 1), (D_MODEL, LANES))
    return pl.pallas_call(
        functools.partial(_diff_proj_kernel, tm=tm, tn=tn),
        grid=(n_out // tn, TOKENS // tm),
        in_specs=[pl.BlockSpec((tm, D_MODEL), lambda n, m: (m, 0)),
                  pl.BlockSpec((parts, tm, LANES), lambda n, m: (0, m, 0)),
                  pl.BlockSpec((D_MODEL, LANES), lambda n, m: (0, 0)),
                  pl.BlockSpec((None, D_MODEL, tn), lambda n, m: (layer, 0, n))],
        out_specs=pl.BlockSpec((tm, tn), lambda n, m: (m, n)),
        out_shape=jax.ShapeDtypeStruct((TOKENS, n_out), BF16),
        scratch_shapes=[pltpu.VMEM((D_MODEL, tn), BF16)],
        compiler_params=_params("arbitrary", "arbitrary"),
        name="diff_proj",
    )(hb, ssq, g_rows, w_in)


def _residual_matmul_kernel(x_ref, w_ref, h_ref, o_ref, *rest, tm, emit_stats):
    wb_ref = rest[-1]

    @pl.when(pl.program_id(1) == 0)
    def _():
        wb_ref[...] = w_ref[...].astype(BF16)

    for r in _row_blocks(tm):
        rows = slice(r, r + SUB_ROWS)
        acc = jnp.dot(x_ref[rows, :], wb_ref[...], preferred_element_type=F32)
        hn = h_ref[rows, :] + acc
        o_ref[rows, :] = hn
        if emit_stats:
            hb_ref, ss_ref = rest[:2]
            hb_ref[rows, :] = hn.astype(BF16)
            ssq = jnp.sum(hn * hn, axis=-1, keepdims=True)
            ss_ref[rows, :] = jnp.broadcast_to(ssq, (SUB_ROWS, LANES))


def _residual_matmul(x, w, layer, h, tiles, emit_stats=False):
    tm, tn = MM_TILES[tiles]
    k = x.shape[1]
    nt = D_MODEL // tn
    w_mode = {"pipeline_mode": pl.Buffered(1)} if 2 * k * tn * 4 > VMEM_LIMIT // 2 else {}
    tile = pl.BlockSpec((tm, tn), lambda n, m: (m, n))
    out_specs = [tile]
    out_shape = [jax.ShapeDtypeStruct((TOKENS, D_MODEL), F32)]
    if emit_stats:
        out_specs += [tile, pl.BlockSpec((None, tm, LANES), lambda n, m: (n, m, 0))]
        out_shape += [jax.ShapeDtypeStruct((TOKENS, D_MODEL), BF16),
                      jax.ShapeDtypeStruct((nt, TOKENS, LANES), F32)]
    outs = pl.pallas_call(
        functools.partial(_residual_matmul_kernel, tm=tm, emit_stats=emit_stats),
        grid=(nt, TOKENS // tm),
        in_specs=[pl.BlockSpec((tm, k), lambda n, m: (m, 0)),
                  pl.BlockSpec((None, k, tn), lambda n, m: (layer, 0, n), **w_mode),
                  tile],
        out_specs=out_specs,
        out_shape=out_shape,
        scratch_shapes=[pltpu.VMEM((k, tn), BF16)],
        compiler_params=_params("arbitrary", "arbitrary"),
        name="residual_matmul",
    )(x, w, h)
    return outs if emit_stats else outs[0]


CARRY_ROWS = 8


def _ffn_in_kernel(x_ref, ss_ref, gr_ref, wg_ref, wu_ref, cw_ref, cb_ref, o_ref,
                   wgb_ref, wub_ref, gs_ref, *, tm, tn):
    m = pl.program_id(1)

    @pl.when(m == 0)
    def _():
        gain = _lane_tile(gr_ref[...], tn)
        wgb_ref[...] = (wg_ref[...] * gain).astype(BF16)
        wub_ref[...] = (wu_ref[...] * gain).astype(BF16)

    @pl.when(m % (SEQ // tm) == 0)
    def _():
        gs_ref[0:CARRY_ROWS, :] = jnp.zeros((CARRY_ROWS, gs_ref.shape[1]), F32)

    for r in _row_blocks(tm):
        rows = slice(r, r + SUB_ROWS)
        x = x_ref[rows, :]
        ssq = ss_ref[0, rows, :]
        for t in range(1, ss_ref.shape[0]):
            ssq = ssq + ss_ref[t, rows, :]
        rs = _lane_tile(lax.rsqrt(ssq * (1.0 / D_MODEL) + NORM_EPS), tn)
        g = jnp.dot(x, wgb_ref[...], preferred_element_type=F32) * rs
        u = jnp.dot(x, wub_ref[...], preferred_element_type=F32) * rs
        base = CARRY_ROWS + r
        gs_ref[base:base + SUB_ROWS, :] = g
        g1 = gs_ref[base - 1:base - 1 + SUB_ROWS, :]
        g2 = gs_ref[base - 2:base - 2 + SUB_ROWS, :]
        conv = (cw_ref[2:3, :] * g + cw_ref[1:2, :] * g1 + cw_ref[0:1, :] * g2
                + cb_ref[...])
        gelu = 0.5 * conv * (1.0 + lax.erf(conv * (2.0 ** -0.5)))
        o_ref[r:r + SUB_ROWS, :] = (gelu * u).astype(BF16)
    gs_ref[0:CARRY_ROWS, :] = gs_ref[tm:tm + CARRY_ROWS, :]


def _ffn_in(hb, ssq, g, w_in, layer, conv_w, conv_b):
    tm, tn = MM_TILES["ffn_in"]
    nt = D_FF // tn
    parts = ssq.shape[0]
    g_rows = jnp.broadcast_to(g.reshape(D_MODEL, 1), (D_MODEL, LANES))
    return pl.pallas_call(
        functools.partial(_ffn_in_kernel, tm=tm, tn=tn),
        grid=(nt, TOKENS // tm),
        in_specs=[pl.BlockSpec((tm, D_MODEL), lambda n, m: (m, 0)),
                  pl.BlockSpec((parts, tm, LANES), lambda n, m: (0, m, 0)),
                  pl.BlockSpec((D_MODEL, LANES), lambda n, m: (0, 0)),
                  pl.BlockSpec((None, D_MODEL, tn), lambda n, m: (layer, 0, n)),
                  pl.BlockSpec((None, D_MODEL, tn), lambda n, m: (layer, 0, nt + n)),
                  pl.BlockSpec((CONV_WIDTH, tn), lambda n, m: (0, n)),
                  pl.BlockSpec((1, tn), lambda n, m: (0, n))],
        out_specs=pl.BlockSpec((tm, tn), lambda n, m: (m, n)),
        out_shape=jax.ShapeDtypeStruct((TOKENS, D_FF), BF16),
        scratch_shapes=[pltpu.VMEM((D_MODEL, tn), BF16),
                        pltpu.VMEM((D_MODEL, tn), BF16),
                        pltpu.VMEM((tm + CARRY_ROWS, tn), F32)],
        compiler_params=_params("arbitrary", "arbitrary"),
        name="ffn_in",
    )(hb, ssq, g_rows, w_in, w_in, conv_w, conv_b.reshape(1, D_FF))


def _split3(x):
    hi = x.astype(BF16)
    r1 = x - hi.astype(F32)
    mid = r1.astype(BF16)
    lo = (r1 - mid.astype(F32)).astype(BF16)
    return jnp.concatenate([hi, mid, lo], axis=-1)


def _split3_packed(c):
    hi = c.astype(BF16).astype(F32)
    r1 = c - hi
    mid = r1.astype(BF16).astype(F32)
    lo = r1 - mid
    lane = lax.broadcasted_iota(jnp.int32, c.shape, 1)
    packed = jnp.where(lane < FOX_HEADS, hi,
                       jnp.where(lane < 2 * FOX_HEADS, pltpu.roll(mid, FOX_HEADS, 1),
                                 jnp.where(lane < 3 * FOX_HEADS,
                                           pltpu.roll(lo, 2 * FOX_HEADS, 1), 0.0)))
    return packed.astype(BF16)


CUM_BLOCK = 512


DECAY_ROWS = 1024


def _fox_decay_kernel(h_ref, g_ref, wt_ref, b_ref, sel_ref, ones_ref,
                      a_ref, cols_ref, c_ref, carry_ref):
    @pl.when(pl.program_id(1) == 0)
    def _():
        carry_ref[...] = jnp.zeros(carry_ref.shape, F32)

    wt = jnp.concatenate([wt_ref[...], jnp.zeros((LANES - FOX_HEADS, D_MODEL), F32)],
                         axis=0).astype(BF16)
    row = lax.broadcasted_iota(jnp.int32, (CUM_BLOCK, CUM_BLOCK), 0)
    col = lax.broadcasted_iota(jnp.int32, (CUM_BLOCK, CUM_BLOCK), 1)
    tri = jnp.where(row >= col, 1.0, 0.0).astype(BF16)
    carry = carry_ref[0:1, :]
    for blk in range(DECAY_ROWS // CUM_BLOCK):
        sl = slice(blk * CUM_BLOCK, (blk + 1) * CUM_BLOCK)
        x = h_ref[sl, :]
        ms = jnp.mean(x * x, axis=-1, keepdims=True)
        a = (x * lax.rsqrt(ms + NORM_EPS) * g_ref[...]).astype(BF16)
        a_ref[sl, :] = a
        z = lax.dot_general(a, wt, _NT, preferred_element_type=F32) + b_ref[...]
        logf = (jnp.minimum(z, 0.0) - jnp.log1p(jnp.exp(-jnp.abs(z)))) * LOG2E
        cs = jnp.dot(tri, _split3(logf), preferred_element_type=F32)
        c_blk = (cs[:, :LANES] + cs[:, LANES:2 * LANES] + cs[:, 2 * LANES:]) + carry
        carry = c_blk[CUM_BLOCK - 1:CUM_BLOCK, :]
        c_ref[sl, :] = c_blk
    carry_ref[0:1, :] = carry
    parts = _split3_packed(c_ref[...])
    cols = jnp.dot(parts, sel_ref[...], preferred_element_type=F32) + ones_ref[...]
    cols_ref[...] = cols.astype(BF16)


N_TERMS = 3


def _decay_selectors():
    sel = np.zeros((LANES, D_MODEL), np.float32)
    ones = np.zeros((1, D_MODEL), np.float32)
    for h in range(FOX_HEADS):
        for p in range(N_TERMS):
            sel[p * FOX_HEADS + h, h * HEAD_DIM + p] = 1.0
            ones[0, h * HEAD_DIM + N_TERMS + p] = 1.0
    return jnp.asarray(sel, BF16), jnp.asarray(ones)


def _fox_norm_decay(h, g, w_in_t, layer, b_f):
    b_pad = jnp.pad(b_f, (0, LANES - FOX_HEADS)).reshape(1, LANES)
    sel, ones = _decay_selectors()
    steps = SEQ // DECAY_ROWS
    full = lambda shape: pl.BlockSpec(shape, lambda b, s: (0, 0))
    rows = pl.BlockSpec((DECAY_ROWS, D_MODEL), lambda b, s: (b * steps + s, 0))
    out = jax.ShapeDtypeStruct((TOKENS, D_MODEL), BF16)
    gate_block = 4 * D_MODEL // FOX_HEADS
    return pl.pallas_call(
        _fox_decay_kernel,
        grid=(BATCH, steps),
        in_specs=[rows, full((1, D_MODEL)),
                  pl.BlockSpec((None, FOX_HEADS, D_MODEL),
                               lambda b, s: (layer, gate_block, 0)),
                  full((1, LANES)),
                  full((LANES, D_MODEL)), full((1, D_MODEL))],
        out_specs=[rows, rows],
        out_shape=[out, out],
        scratch_shapes=[pltpu.VMEM((DECAY_ROWS, LANES), F32),
                        pltpu.VMEM((8, LANES), F32)],
        compiler_params=_params("arbitrary", "arbitrary"),
        name="fox_decay",
    )(h, g.reshape(1, D_MODEL), w_in_t, b_pad, sel, ones)


_NT = (((1,), (1,)), ((), ()))


def _transpose_values(v_ref, lanes, vt_ref):
    for c in range(0, SEQ, 512):
        vt_ref[:, c:c + 512] = v_ref[0, c:c + 512, lanes].astype(F32).T.astype(BF16)


def _softmax_columns(sr, pr, n_tiles):
    m = jnp.max(sr[:n_tiles * TK, :], axis=0, keepdims=True)
    l = jnp.zeros((1, TQ), F32)
    for j in range(n_tiles):
        rows = slice(j * TK, (j + 1) * TK)
        p = jnp.exp2(sr[rows, :] - m)
        l = l + jnp.sum(p, axis=0, keepdims=True)
        pr[rows, :] = p.astype(BF16)
    return l


def _staged(n_units, stage_scores, stage_softmax, stage_output):
    stages = ((stage_scores, 0), (stage_softmax, 2), (stage_output, 4))
    for t in range(n_units + 4):
        for stage, lag in stages:
            if 0 <= t - lag < n_units:
                stage(t - lag)


def _fox_attn_kernel(q_ref, c_ref, k_ref, v_ref, g_ref, o_ref,
                     s_ref, p_ref, vt_ref, cq_ref):
    krow = lax.broadcasted_iota(jnp.int32, (TK, TQ), 0)
    qcol = lax.broadcasted_iota(jnp.int32, (TK, TQ), 1)
    causal = krow <= qcol
    lane = lax.broadcasted_iota(jnp.int32, (512, HEAD_DIM), 1)
    head_lanes = [slice(hh * HEAD_DIM, (hh + 1) * HEAD_DIM) for hh in range(FOX_HPS)]
    for hh in range(FOX_HPS):
        _transpose_values(v_ref, head_lanes[hh], vt_ref.at[hh])
        for c in range(0, SEQ, 512):
            cols = c_ref[0, c:c + 512, head_lanes[hh]].astype(F32)
            cq = jnp.where(lane < N_TERMS, -1.0,
                           jnp.where(lane < 2 * N_TERMS, pltpu.roll(cols, N_TERMS, 1), 0.0))
            cq_ref[hh, c:c + 512, :] = cq.astype(BF16)
    units = [(hh, i) for i in range(NQ) for hh in range(FOX_HPS)]
    sums = {}

    def stage_scores(u):
        hh, i = units[u]
        hl = head_lanes[hh]
        qs, keys = i * TQ, i * TQ + TK
        sr = s_ref.at[u % FOX_SLOTS]
        qa = jnp.concatenate([q_ref[0, qs:qs + TQ, hl], cq_ref[hh, qs:qs + TQ, :]], axis=-1)
        ka = jnp.concatenate([k_ref[0, :keys, hl], c_ref[0, :keys, hl]], axis=-1)
        s = lax.dot_general(ka, qa, _NT, preferred_element_type=F32)
        if i > 0:
            sr[:qs, :] = s[:qs]
        sr[qs:keys, :] = jnp.where(causal, s[qs:keys], NEG)

    def stage_softmax(u):
        _, i = units[u]
        sums[u] = _softmax_columns(s_ref.at[u % FOX_SLOTS], p_ref.at[u % FOX_SLOTS], i + 1)

    def stage_output(u):
        hh, i = units[u]
        hl = head_lanes[hh]
        qs, keys = i * TQ, i * TQ + TK
        pr = p_ref.at[u % FOX_SLOTS]
        acc_t = jnp.dot(vt_ref[hh, :, :keys], pr[:keys, :], preferred_element_type=F32)
        out = (acc_t * (1.0 / sums.pop(u))).T
        gate = g_ref[0, qs:qs + TQ, hl].astype(F32)
        o_ref[0, qs:qs + TQ, hl] = (out * gate).astype(BF16)

    _staged(len(units), stage_scores, stage_softmax, stage_output)


def _fox_attention(proj, cols):
    width = FOX_HPS * HEAD_DIM
    groups = FOX_HEADS // FOX_HPS
    blk = lambda off: pl.BlockSpec((1, SEQ, width), lambda b, h: (b, 0, off * groups + h))
    return pl.pallas_call(
        _fox_attn_kernel,
        grid=(BATCH, groups),
        in_specs=[blk(0), blk(0), blk(1), blk(2), blk(3)],
        out_specs=blk(0),
        out_shape=jax.ShapeDtypeStruct((BATCH, SEQ, D_MODEL), BF16),
        scratch_shapes=[pltpu.VMEM((FOX_SLOTS, SEQ, TQ), F32),
                        pltpu.VMEM((FOX_SLOTS, SEQ, TQ), BF16),
                        pltpu.VMEM((FOX_HPS, HEAD_DIM, SEQ), BF16),
                        pltpu.VMEM((FOX_HPS, SEQ, HEAD_DIM), BF16)],
        compiler_params=_params("arbitrary", "arbitrary"),
        name="fox_attn",
    )(proj, cols, proj, proj, proj)


def _bias_tile_kernel(rb_ref, o_ref):
    mp = pl.program_id(0)
    krow = lax.broadcasted_iota(jnp.int32, (TK, TQ), 0)
    qcol = lax.broadcasted_iota(jnp.int32, (TK, TQ), 1)
    max_exact = NUM_BUCKETS // 2
    for t in range(2):
        n = jnp.maximum(qcol - krow + t * TK, 0)
        nf = jnp.maximum(n, 1).astype(F32)
        large = max_exact + (jnp.log(nf / max_exact) / math.log(MAX_DISTANCE / max_exact)
                             * (NUM_BUCKETS - max_exact)).astype(jnp.int32)
        large = jnp.minimum(large, NUM_BUCKETS - 1)
        bucket = jnp.where(n < max_exact, n, large)
        val = jnp.zeros((TK, TQ), F32)
        for bkt in range(NUM_BUCKETS):
            val = jnp.where(bucket == bkt, rb_ref[bkt, mp], val)
        val = (val - rb_ref[NUM_BUCKETS - 1, mp]) * LOG2E
        if t == 0:
            val = jnp.where(krow <= qcol, val, NEG)
        o_ref[0, t] = val


def _bias_tiles(rel_bias):
    n_maps = 2 * DIFF_HEADS
    return pl.pallas_call(
        _bias_tile_kernel,
        grid=(n_maps,),
        in_specs=[pl.BlockSpec(memory_space=pltpu.SMEM)],
        out_specs=pl.BlockSpec((1, 2, TK, TQ), lambda mp: (mp, 0, 0, 0)),
        out_shape=jax.ShapeDtypeStruct((n_maps, 2, TK, TQ), F32),
        compiler_params=_params("arbitrary"),
        name="bias_tiles",
    )(rel_bias.reshape(NUM_BUCKETS, n_maps))


def _diff_attn_kernel(q1_ref, q2_ref, k1_ref, k2_ref, v_ref, bias_ref, lam_ref,
                      sub_ref, o_ref, s_ref, p_ref, vt_ref, *, lambda_init):
    lamf = lam_ref[...]
    lam_full = (jnp.exp(jnp.sum(lamf[0:1] * lamf[1:2], axis=-1, keepdims=True))
                - jnp.exp(jnp.sum(lamf[2:3] * lamf[3:4], axis=-1, keepdims=True))
                + lambda_init)
    _transpose_values(v_ref, slice(None), vt_ref)
    maps = ((q1_ref, k1_ref), (q2_ref, k2_ref))
    sums = {}

    def slot(i, mp):
        return 2 * (i % (DIFF_SLOTS // 2)) + mp

    def stage_scores(i):
        qs, keys = i * TQ, i * TQ + TK
        near = qs - TK
        for mp, (q_ref, k_ref) in enumerate(maps):
            sr = s_ref.at[slot(i, mp)]
            s = lax.dot_general(k_ref[0, :keys, :], q_ref[0, qs:qs + TQ, :], _NT,
                                preferred_element_type=F32)
            if i > 1:
                sr[:near, :] = s[:near]
            if i > 0:
                sr[near:qs, :] = s[near:qs] + bias_ref[mp, 1]
            sr[qs:keys, :] = s[qs:keys] + bias_ref[mp, 0]

    def stage_softmax(i):
        for mp in range(2):
            sl = slot(i, mp)
            sums[(i, mp)] = _softmax_columns(s_ref.at[sl], p_ref.at[sl], i + 1)

    def stage_output(i):
        qs, keys = i * TQ, i * TQ + TK
        outs = []
        for mp in range(2):
            pr = p_ref.at[slot(i, mp)]
            acc_t = jnp.dot(vt_ref[:, :keys], pr[:keys, :], preferred_element_type=F32)
            outs.append(acc_t * (1.0 / sums.pop((i, mp))))
        o = (outs[0] - lam_full * outs[1]).T
        ms = jnp.mean(o * o, axis=-1, keepdims=True)
        o = o * lax.rsqrt(ms + SUBLN_EPS) * sub_ref[...] * (1.0 - lambda_init)
        o_ref[0, qs:qs + TQ, :] = o.astype(BF16)

    _staged(NQ, stage_scores, stage_softmax, stage_output)


def _diff_attention(proj, bias_tiles, lam, subln, lambda_init):
    qk = lambda off, j: pl.BlockSpec((1, SEQ, HEAD_DIM),
                                     lambda h, b: (b, 0, off + 2 * h + j))
    n_maps = 2 * DIFF_HEADS
    width = 2 * HEAD_DIM
    return pl.pallas_call(
        functools.partial(_diff_attn_kernel, lambda_init=lambda_init),
        grid=(DIFF_HEADS, BATCH),
        in_specs=[qk(0, 0), qk(0, 1), qk(n_maps, 0), qk(n_maps, 1),
                  pl.BlockSpec((1, SEQ, width), lambda h, b: (b, 0, 2 * DIFF_HEADS + h)),
                  pl.BlockSpec((2, 2, TK, TQ), lambda h, b: (h, 0, 0, 0)),
                  pl.BlockSpec((4, HEAD_DIM), lambda h, b: (0, 0)),
                  pl.BlockSpec((1, width), lambda h, b: (0, 0))],
        out_specs=pl.BlockSpec((1, SEQ, width), lambda h, b: (b, 0, h)),
        out_shape=jax.ShapeDtypeStruct((BATCH, SEQ, D_MODEL), BF16),
        scratch_shapes=[pltpu.VMEM((DIFF_SLOTS, SEQ, TQ), F32),
                        pltpu.VMEM((DIFF_SLOTS, SEQ, TQ), BF16),
                        pltpu.VMEM((width, SEQ), BF16)],
        compiler_params=_params("arbitrary", "arbitrary"),
        name="diff_attn",
    )(proj, proj, proj, proj, proj, bias_tiles, lam, subln.reshape(1, width))


def kernel(x, attn_norm, ffn_norm, final_norm, fox_w_in, fox_b_f, fox_qk_norm, fox_w_out,
           diff_w_in, diff_lambda, diff_subln, diff_w_out, rel_bias,
           ffn_w_in, ffn_conv_w, ffn_conv_b, ffn_w_out):
    h = x.reshape(TOKENS, D_MODEL)
    bias_tiles = _bias_tiles(rel_bias)
    fox_w_in_t = jnp.swapaxes(fox_w_in, 1, 2)
    for i in range(DEPTH):
        j = i // 2
        if i % 2 == 0:
            a, cols = _fox_norm_decay(h, attn_norm[i], fox_w_in_t, j, fox_b_f[j])
            proj = _fox_proj(a, fox_w_in_t, j, fox_qk_norm[j])
            o = _fox_attention(proj.reshape(BATCH, SEQ, 4 * D_MODEL),
                               cols.reshape(BATCH, SEQ, D_MODEL))
            w_out = fox_w_out
        else:
            lambda_init = 0.8 - 0.6 * math.exp(-0.3 * i)
            proj = _diff_proj(hb, ssq, attn_norm[i], diff_w_in, j)
            o = _diff_attention(proj.reshape(BATCH, SEQ, 3 * D_MODEL), bias_tiles,
                                diff_lambda[j], diff_subln[j], lambda_init)
            w_out = diff_w_out
        h, hb, ssq = _residual_matmul(o.reshape(TOKENS, D_MODEL), w_out, j, h, "attn_out",
                                      emit_stats=True)
        y = _ffn_in(hb, ssq, ffn_norm[i], ffn_w_in, i, ffn_conv_w[i], ffn_conv_b[i])
        if i % 2 == 0:
            h, hb, ssq = _residual_matmul(y, ffn_w_out, i, h, "ffn_out", emit_stats=True)
        else:
            h = _residual_matmul(y, ffn_w_out, i, h, "ffn_out")
    return _rmsnorm(h, final_norm, F32).reshape(BATCH, SEQ, D_MODEL)
```

```python
import functools
import math

import jax
import jax.numpy as jnp
import numpy as np
from jax import lax
from jax.experimental import pallas as pl
from jax.experimental.pallas import tpu as pltpu

D_MODEL = 2048
BATCH = 4
SEQ = 2048
TOKENS = BATCH * SEQ
DEPTH = 4
HEAD_DIM = 128
FOX_HEADS = D_MODEL // HEAD_DIM
DIFF_HEADS = D_MODEL // (2 * HEAD_DIM)
D_FF = 5632
CONV_WIDTH = 3
NUM_BUCKETS = 32
MAX_DISTANCE = 128
NORM_EPS = 1e-6
SUBLN_EPS = 1e-5

LANES = 128
VMEM_LIMIT = 56 * 1024 * 1024
NEG = -1e30
LOG2E = math.log2(math.e)
Q_SCALE = HEAD_DIM ** -0.5 * LOG2E

TQ = 256
TK = 256
NQ = SEQ // TQ
FOX_HPS = 2
FOX_SLOTS = 5
DIFF_SLOTS = 10

SUB_ROWS = 256
MM_TILES = {
    "fox_proj": (2048, 1024),
    "diff_proj": (2048, 1024),
    "attn_out": (512, D_MODEL),
    "ffn_in": (2048, 512),
    "ffn_out": (512, 1024),
}

F32 = jnp.float32
BF16 = jnp.bfloat16


def _params(*sem):
    return pltpu.CompilerParams(dimension_semantics=sem, vmem_limit_bytes=VMEM_LIMIT)


def _rmsnorm_kernel(x_ref, g_ref, o_ref):
    x = x_ref[...]
    ms = jnp.mean(x * x, axis=-1, keepdims=True)
    o_ref[...] = (x * lax.rsqrt(ms + NORM_EPS) * g_ref[...]).astype(o_ref.dtype)


def _rmsnorm(h, g, out_dtype):
    tm = 512
    return pl.pallas_call(
        _rmsnorm_kernel,
        grid=(TOKENS // tm,),
        in_specs=[pl.BlockSpec((tm, D_MODEL), lambda m: (m, 0)),
                  pl.BlockSpec((1, D_MODEL), lambda m: (0, 0))],
        out_specs=pl.BlockSpec((tm, D_MODEL), lambda m: (m, 0)),
        out_shape=jax.ShapeDtypeStruct((TOKENS, D_MODEL), out_dtype),
        compiler_params=_params("arbitrary"),
        name="rmsnorm",
    )(h, g.reshape(1, D_MODEL))


def _head_rmsnorm(acc, gain, mult):
    ms = jnp.mean(acc * acc, axis=-1, keepdims=True)
    return acc * lax.rsqrt(ms + NORM_EPS) * (gain * mult)


def _row_blocks(tm):
    return range(0, tm, SUB_ROWS)


def _fox_proj_kernel(x_ref, wt_ref, qkn_ref, o_ref, wb_ref, *, tm, tn):
    n = pl.program_id(0)

    @pl.when(pl.program_id(1) == 0)
    def _():
        for c in range(0, tn, SUB_ROWS):
            wb_ref[:, c:c + SUB_ROWS] = wt_ref[c:c + SUB_ROWS, :].T.astype(BF16)

    per = D_MODEL // tn

    def body(epilogue):
        for r in _row_blocks(tm):
            rows = slice(r, r + SUB_ROWS)
            acc = jnp.dot(x_ref[rows, :], wb_ref[...], preferred_element_type=F32)
            epilogue(acc, rows)

    def normed(acc, rows):
        gain = qkn_ref[pl.ds(n // per, 1), :]
        mult = jnp.where(n < per, Q_SCALE, 1.0).astype(F32)
        for c in range(tn // HEAD_DIM):
            sl = slice(c * HEAD_DIM, (c + 1) * HEAD_DIM)
            o_ref[rows, sl] = _head_rmsnorm(acc[:, sl], gain, mult).astype(BF16)

    def value_or_gate(acc, rows):
        o_ref[rows, :] = jnp.where(n >= 3 * per, jax.nn.sigmoid(acc), acc).astype(BF16)

    @pl.when(n < 2 * per)
    def _():
        body(normed)

    @pl.when(n >= 2 * per)
    def _():
        body(value_or_gate)


def _fox_proj(a, w_in_t, layer, qk_norm):
    tm, tn = MM_TILES["fox_proj"]
    n_out = 4 * D_MODEL
    return pl.pallas_call(
        functools.partial(_fox_proj_kernel, tm=tm, tn=tn),
        grid=(n_out // tn, TOKENS // tm),
        in_specs=[pl.BlockSpec((tm, D_MODEL), lambda n, m: (m, 0)),
                  pl.BlockSpec((None, tn, D_MODEL), lambda n, m: (layer, n, 0)),
                  pl.BlockSpec((2, HEAD_DIM), lambda n, m: (0, 0))],
        out_specs=pl.BlockSpec((tm, tn), lambda n, m: (m, n)),
        out_shape=jax.ShapeDtypeStruct((TOKENS, n_out), BF16),
        scratch_shapes=[pltpu.VMEM((D_MODEL, tn), BF16)],
        compiler_params=_params("arbitrary", "arbitrary"),
        name="fox_proj",
    )(a, w_in_t, qk_norm)


def _lane_tile(x, width):
    return jnp.concatenate([x] * (width // LANES), axis=1)


def _diff_proj_kernel(x_ref, ss_ref, g_ref, w_ref, o_ref, wb_ref, *, tm, tn):
    n = pl.program_id(0)

    @pl.when(pl.program_id(1) == 0)
    def _():
        wb_ref[...] = (w_ref[...] * _lane_tile(g_ref[...], tn)).astype(BF16)

    mult = jnp.where(n < D_MODEL // tn, Q_SCALE, 1.0).astype(F32)
    for r in _row_blocks(tm):
        rows = slice(r, r + SUB_ROWS)
        acc = jnp.dot(x_ref[rows, :], wb_ref[...], preferred_element_type=F32)
        ssq = ss_ref[0, rows, :]
        for t in range(1, ss_ref.shape[0]):
            ssq = ssq + ss_ref[t, rows, :]
        rs = lax.rsqrt(ssq * (1.0 / D_MODEL) + NORM_EPS) * mult
        o_ref[rows, :] = (acc * _lane_tile(rs, tn)).astype(BF16)


def _diff_proj(hb, ssq, g, w_in, layer):
    tm, tn = MM_TILES["diff_proj"]
    n_out = 3 * D_MODEL
    parts = ssq.shape[0]
    g_rows = jnp.broadcast_to(g.reshape(D_MODEL, 1), (D_MODEL, LANES))
    return pl.pallas_call(
        functools.partial(_diff_proj_kernel, tm=tm, tn=tn),
        grid=(n_out // tn, TOKENS // tm),
        in_specs=[pl.BlockSpec((tm, D_MODEL), lambda n, m: (m, 0)),
                  pl.BlockSpec((parts, tm, LANES), lambda n, m: (0, m, 0)),
                  pl.BlockSpec((D_MODEL, LANES), lambda n, m: (0, 0)),
                  pl.BlockSpec((None, D_MODEL, tn), lambda n, m: (layer, 0, n))],
        out_specs=pl.BlockSpec((tm, tn), lambda n, m: (m, n)),
        out_shape=jax.ShapeDtypeStruct((TOKENS, n_out), BF16),
        scratch_shapes=[pltpu.VMEM((D_MODEL, tn), BF16)],
        compiler_params=_params("arbitrary", "arbitrary"),
        name="diff_proj",
    )(hb, ssq, g_rows, w_in)


def _residual_matmul_kernel(x_ref, w_ref, h_ref, o_ref, *rest, tm, emit_stats):
    wb_ref = rest[-1]

    @pl.when(pl.program_id(1) == 0)
    def _():
        wb_ref[...] = w_ref[...].astype(BF16)

    for r in _row_blocks(tm):
        rows = slice(r, r + SUB_ROWS)
        acc = jnp.dot(x_ref[rows, :], wb_ref[...], preferred_element_type=F32)
        hn = h_ref[rows, :] + acc
        o_ref[rows, :] = hn
        if emit_stats:
            hb_ref, ss_ref = rest[:2]
            hb_ref[rows, :] = hn.astype(BF16)
            ssq = jnp.sum(hn * hn, axis=-1, keepdims=True)
            ss_ref[rows, :] = jnp.broadcast_to(ssq, (SUB_ROWS, LANES))


def _residual_matmul(x, w, layer, h, tiles, emit_stats=False):
    tm, tn = MM_TILES[tiles]
    k = x.shape[1]
    nt = D_MODEL // tn
    tile = pl.BlockSpec((tm, tn), lambda n, m: (m, n))
    out_specs = [tile]
    out_shape = [jax.ShapeDtypeStruct((TOKENS, D_MODEL), F32)]
    if emit_stats:
        out_specs += [tile, pl.BlockSpec((None, tm, LANES), lambda n, m: (n, m, 0))]
        out_shape += [jax.ShapeDtypeStruct((TOKENS, D_MODEL), BF16),
                      jax.ShapeDtypeStruct((nt, TOKENS, LANES), F32)]
    outs = pl.pallas_call(
        functools.partial(_residual_matmul_kernel, tm=tm, emit_stats=emit_stats),
        grid=(nt, TOKENS // tm),
        in_specs=[pl.BlockSpec((tm, k), lambda n, m: (m, 0)),
                  pl.BlockSpec((None, k, tn), lambda n, m: (layer, 0, n),
                               pipeline_mode=pl.Buffered(1)),
                  tile],
        out_specs=out_specs,
        out_shape=out_shape,
        scratch_shapes=[pltpu.VMEM((k, tn), BF16)],
        compiler_params=_params("arbitrary", "arbitrary"),
        name="residual_matmul",
    )(x, w, h)
    return outs if emit_stats else outs[0]


def _attn_out_kernel(x_ref, w_ref, h_ref, g_ref, o_ref, a_ref, wb_ref, *, tm):
    @pl.when(pl.program_id(0) == 0)
    def _():
        for c in range(0, D_MODEL, 512):
            wb_ref[c:c + 512, :] = w_ref[c:c + 512, :].astype(BF16)

    for r in _row_blocks(tm):
        rows = slice(r, r + SUB_ROWS)
        acc = jnp.dot(x_ref[rows, :], wb_ref[...], preferred_element_type=F32)
        hn = h_ref[rows, :] + acc
        o_ref[rows, :] = hn
        ms = jnp.mean(hn * hn, axis=-1, keepdims=True)
        a_ref[rows, :] = (hn * lax.rsqrt(ms + NORM_EPS) * g_ref[...]).astype(BF16)


def _attn_out(x, w, layer, h, g):
    tm, tn = MM_TILES["attn_out"]
    assert tn == D_MODEL
    rows = lambda m: (m, 0)
    return pl.pallas_call(
        functools.partial(_attn_out_kernel, tm=tm),
        grid=(TOKENS // tm,),
        in_specs=[pl.BlockSpec((tm, D_MODEL), rows),
                  pl.BlockSpec((None, D_MODEL, D_MODEL), lambda m: (layer, 0, 0),
                               pipeline_mode=pl.Buffered(1)),
                  pl.BlockSpec((tm, D_MODEL), rows),
                  pl.BlockSpec((1, D_MODEL), lambda m: (0, 0))],
        out_specs=[pl.BlockSpec((tm, D_MODEL), rows), pl.BlockSpec((tm, D_MODEL), rows)],
        out_shape=[jax.ShapeDtypeStruct((TOKENS, D_MODEL), F32),
                   jax.ShapeDtypeStruct((TOKENS, D_MODEL), BF16)],
        scratch_shapes=[pltpu.VMEM((D_MODEL, D_MODEL), BF16)],
        compiler_params=_params("arbitrary"),
        name="attn_out",
    )(x, w, h, g.reshape(1, D_MODEL))


CARRY_ROWS = 8


def _ffn_in_kernel(x_ref, wg_ref, wu_ref, cw_ref, cb_ref, o_ref,
                   wgb_ref, wub_ref, gs_ref, *, tm):
    m = pl.program_id(1)

    @pl.when(m == 0)
    def _():
        wgb_ref[...] = wg_ref[...].astype(BF16)
        wub_ref[...] = wu_ref[...].astype(BF16)

    @pl.when(m % (SEQ // tm) == 0)
    def _():
        gs_ref[0:CARRY_ROWS, :] = jnp.zeros((CARRY_ROWS, gs_ref.shape[1]), F32)

    for r in _row_blocks(tm):
        x = x_ref[r:r + SUB_ROWS, :]
        g = jnp.dot(x, wgb_ref[...], preferred_element_type=F32)
        u = jnp.dot(x, wub_ref[...], preferred_element_type=F32)
        base = CARRY_ROWS + r
        gs_ref[base:base + SUB_ROWS, :] = g
        g1 = gs_ref[base - 1:base - 1 + SUB_ROWS, :]
        g2 = gs_ref[base - 2:base - 2 + SUB_ROWS, :]
        conv = (cw_ref[2:3, :] * g + cw_ref[1:2, :] * g1 + cw_ref[0:1, :] * g2
                + cb_ref[...])
        gelu = 0.5 * conv * (1.0 + lax.erf(conv * (2.0 ** -0.5)))
        o_ref[r:r + SUB_ROWS, :] = (gelu * u).astype(BF16)
    gs_ref[0:CARRY_ROWS, :] = gs_ref[tm:tm + CARRY_ROWS, :]


def _ffn_in(a, w_in, layer, conv_w, conv_b):
    tm, tn = MM_TILES["ffn_in"]
    nt = D_FF // tn
    return pl.pallas_call(
        functools.partial(_ffn_in_kernel, tm=tm),
        grid=(nt, TOKENS // tm),
        in_specs=[pl.BlockSpec((tm, D_MODEL), lambda n, m: (m, 0)),
                  pl.BlockSpec((None, D_MODEL, tn), lambda n, m: (layer, 0, n)),
                  pl.BlockSpec((None, D_MODEL, tn), lambda n, m: (layer, 0, nt + n)),
                  pl.BlockSpec((CONV_WIDTH, tn), lambda n, m: (0, n)),
                  pl.BlockSpec((1, tn), lambda n, m: (0, n))],
        out_specs=pl.BlockSpec((tm, tn), lambda n, m: (m, n)),
        out_shape=jax.ShapeDtypeStruct((TOKENS, D_FF), BF16),
        scratch_shapes=[pltpu.VMEM((D_MODEL, tn), BF16),
                        pltpu.VMEM((D_MODEL, tn), BF16),
                        pltpu.VMEM((tm + CARRY_ROWS, tn), F32)],
        compiler_params=_params("arbitrary", "arbitrary"),
        name="ffn_in",
    )(a, w_in, w_in, conv_w, conv_b.reshape(1, D_FF))


def _split3(x):
    hi = x.astype(BF16)
    r1 = x - hi.astype(F32)
    mid = r1.astype(BF16)
    lo = (r1 - mid.astype(F32)).astype(BF16)
    return jnp.concatenate([hi, mid, lo], axis=-1)


def _split3_packed(c):
    hi = c.astype(BF16).astype(F32)
    r1 = c - hi
    mid = r1.astype(BF16).astype(F32)
    lo = r1 - mid
    lane = lax.broadcasted_iota(jnp.int32, c.shape, 1)
    packed = jnp.where(lane < FOX_HEADS, hi,
                       jnp.where(lane < 2 * FOX_HEADS, pltpu.roll(mid, FOX_HEADS, 1),
                                 jnp.where(lane < 3 * FOX_HEADS,
                                           pltpu.roll(lo, 2 * FOX_HEADS, 1), 0.0)))
    return packed.astype(BF16)


CUM_BLOCK = 512


DECAY_ROWS = 1024


def _fox_decay_kernel(h_ref, g_ref, wt_ref, b_ref, sel_ref, ones_ref,
                      a_ref, cols_ref, c_ref, carry_ref):
    @pl.when(pl.program_id(1) == 0)
    def _():
        carry_ref[...] = jnp.zeros(carry_ref.shape, F32)

    wt = jnp.concatenate([wt_ref[...], jnp.zeros((LANES - FOX_HEADS, D_MODEL), F32)],
                         axis=0).astype(BF16)
    row = lax.broadcasted_iota(jnp.int32, (CUM_BLOCK, CUM_BLOCK), 0)
    col = lax.broadcasted_iota(jnp.int32, (CUM_BLOCK, CUM_BLOCK), 1)
    tri = jnp.where(row >= col, 1.0, 0.0).astype(BF16)
    carry = carry_ref[0:1, :]
    for blk in range(DECAY_ROWS // CUM_BLOCK):
        sl = slice(blk * CUM_BLOCK, (blk + 1) * CUM_BLOCK)
        x = h_ref[sl, :]
        ms = jnp.mean(x * x, axis=-1, keepdims=True)
        a = (x * lax.rsqrt(ms + NORM_EPS) * g_ref[...]).astype(BF16)
        a_ref[sl, :] = a
        z = lax.dot_general(a, wt, _NT, preferred_element_type=F32) + b_ref[...]
        logf = (jnp.minimum(z, 0.0) - jnp.log1p(jnp.exp(-jnp.abs(z)))) * LOG2E
        cs = jnp.dot(tri, _split3(logf), preferred_element_type=F32)
        c_blk = (cs[:, :LANES] + cs[:, LANES:2 * LANES] + cs[:, 2 * LANES:]) + carry
        carry = c_blk[CUM_BLOCK - 1:CUM_BLOCK, :]
        c_ref[sl, :] = c_blk
    carry_ref[0:1, :] = carry
    parts = _split3_packed(c_ref[...])
    cols = jnp.dot(parts, sel_ref[...], preferred_element_type=F32) + ones_ref[...]
    cols_ref[...] = cols.astype(BF16)


N_TERMS = 3


def _decay_selectors():
    sel = np.zeros((LANES, D_MODEL), np.float32)
    ones = np.zeros((1, D_MODEL), np.float32)
    for h in range(FOX_HEADS):
        for p in range(N_TERMS):
            sel[p * FOX_HEADS + h, h * HEAD_DIM + p] = 1.0
            ones[0, h * HEAD_DIM + N_TERMS + p] = 1.0
    return jnp.asarray(sel, BF16), jnp.asarray(ones)


def _fox_norm_decay(h, g, w_in_t, layer, b_f):
    b_pad = jnp.pad(b_f, (0, LANES - FOX_HEADS)).reshape(1, LANES)
    sel, ones = _decay_selectors()
    steps = SEQ // DECAY_ROWS
    full = lambda shape: pl.BlockSpec(shape, lambda b, s: (0, 0))
    rows = pl.BlockSpec((DECAY_ROWS, D_MODEL), lambda b, s: (b * steps + s, 0))
    out = jax.ShapeDtypeStruct((TOKENS, D_MODEL), BF16)
    gate_block = 4 * D_MODEL // FOX_HEADS
    return pl.pallas_call(
        _fox_decay_kernel,
        grid=(BATCH, steps),
        in_specs=[rows, full((1, D_MODEL)),
                  pl.BlockSpec((None, FOX_HEADS, D_MODEL),
                               lambda b, s: (layer, gate_block, 0)),
                  full((1, LANES)),
                  full((LANES, D_MODEL)), full((1, D_MODEL))],
        out_specs=[rows, rows],
        out_shape=[out, out],
        scratch_shapes=[pltpu.VMEM((DECAY_ROWS, LANES), F32),
                        pltpu.VMEM((8, LANES), F32)],
        compiler_params=_params("arbitrary", "arbitrary"),
        name="fox_decay",
    )(h, g.reshape(1, D_MODEL), w_in_t, b_pad, sel, ones)


_NT = (((1,), (1,)), ((), ()))


def _transpose_values(v_ref, lanes, vt_ref):
    for c in range(0, SEQ, 512):
        vt_ref[:, c:c + 512] = v_ref[0, c:c + 512, lanes].astype(F32).T.astype(BF16)


def _softmax_columns(sr, pr, n_tiles):
    m = jnp.max(sr[:n_tiles * TK, :], axis=0, keepdims=True)
    l = jnp.zeros((1, TQ), F32)
    for j in range(n_tiles):
        rows = slice(j * TK, (j + 1) * TK)
        p = jnp.exp2(sr[rows, :] - m)
        l = l + jnp.sum(p, axis=0, keepdims=True)
        pr[rows, :] = p.astype(BF16)
    return l


def _staged(order, stage_scores, stage_softmax, stage_output):
    stages = ((stage_scores, 0), (stage_softmax, 2), (stage_output, 4))
    for t in range(len(order) + 4):
        for stage, lag in stages:
            if 0 <= t - lag < len(order):
                stage(order[t - lag])


def _fox_attn_kernel(q_ref, c_ref, k_ref, v_ref, g_ref, o_ref,
                     s_ref, p_ref, vt_ref, cq_ref):
    krow = lax.broadcasted_iota(jnp.int32, (TK, TQ), 0)
    qcol = lax.broadcasted_iota(jnp.int32, (TK, TQ), 1)
    causal = krow <= qcol
    lane = lax.broadcasted_iota(jnp.int32, (512, HEAD_DIM), 1)
    head_lanes = [slice(hh * HEAD_DIM, (hh + 1) * HEAD_DIM) for hh in range(FOX_HPS)]
    for hh in range(FOX_HPS):
        _transpose_values(v_ref, head_lanes[hh], vt_ref.at[hh])
        for c in range(0, SEQ, 512):
            cols = c_ref[0, c:c + 512, head_lanes[hh]].astype(F32)
            cq = jnp.where(lane < N_TERMS, -1.0,
                           jnp.where(lane < 2 * N_TERMS, pltpu.roll(cols, N_TERMS, 1), 0.0))
            cq_ref[hh, c:c + 512, :] = cq.astype(BF16)
    units = [(hh, i) for i in reversed(range(NQ)) for hh in range(FOX_HPS)]
    sums = {}

    def stage_scores(u):
        hh, i = units[u]
        hl = head_lanes[hh]
        qs, keys = i * TQ, i * TQ + TK
        sr = s_ref.at[u % FOX_SLOTS]
        qa = jnp.concatenate([q_ref[0, qs:qs + TQ, hl], cq_ref[hh, qs:qs + TQ, :]], axis=-1)
        ka = jnp.concatenate([k_ref[0, :keys, hl], c_ref[0, :keys, hl]], axis=-1)
        s = lax.dot_general(ka, qa, _NT, preferred_element_type=F32)
        if i > 0:
            sr[:qs, :] = s[:qs]
        sr[qs:keys, :] = jnp.where(causal, s[qs:keys], NEG)

    def stage_softmax(u):
        _, i = units[u]
        sums[u] = _softmax_columns(s_ref.at[u % FOX_SLOTS], p_ref.at[u % FOX_SLOTS], i + 1)

    def stage_output(u):
        hh, i = units[u]
        hl = head_lanes[hh]
        qs, keys = i * TQ, i * TQ + TK
        pr = p_ref.at[u % FOX_SLOTS]
        acc_t = jnp.dot(vt_ref[hh, :, :keys], pr[:keys, :], preferred_element_type=F32)
        out = (acc_t * (1.0 / sums.pop(u))).T
        gate = g_ref[0, qs:qs + TQ, hl].astype(F32)
        o_ref[0, qs:qs + TQ, hl] = (out * gate).astype(BF16)

    _staged(range(len(units)), stage_scores, stage_softmax, stage_output)


def _fox_attention(proj, cols):
    width = FOX_HPS * HEAD_DIM
    groups = FOX_HEADS // FOX_HPS
    blk = lambda off: pl.BlockSpec((1, SEQ, width), lambda b, h: (b, 0, off * groups + h))
    return pl.pallas_call(
        _fox_attn_kernel,
        grid=(BATCH, groups),
        in_specs=[blk(0), blk(0), blk(1), blk(2), blk(3)],
        out_specs=blk(0),
        out_shape=jax.ShapeDtypeStruct((BATCH, SEQ, D_MODEL), BF16),
        scratch_shapes=[pltpu.VMEM((FOX_SLOTS, SEQ, TQ), F32),
                        pltpu.VMEM((FOX_SLOTS, SEQ, TQ), BF16),
                        pltpu.VMEM((FOX_HPS, HEAD_DIM, SEQ), BF16),
                        pltpu.VMEM((FOX_HPS, SEQ, HEAD_DIM), BF16)],
        compiler_params=_params("arbitrary", "arbitrary"),
        name="fox_attn",
    )(proj, cols, proj, proj, proj)


def _bias_tile_kernel(rb_ref, o_ref):
    mp = pl.program_id(0)
    krow = lax.broadcasted_iota(jnp.int32, (TK, TQ), 0)
    qcol = lax.broadcasted_iota(jnp.int32, (TK, TQ), 1)
    max_exact = NUM_BUCKETS // 2
    for t in range(2):
        n = jnp.maximum(qcol - krow + t * TK, 0)
        nf = jnp.maximum(n, 1).astype(F32)
        large = max_exact + (jnp.log(nf / max_exact) / math.log(MAX_DISTANCE / max_exact)
                             * (NUM_BUCKETS - max_exact)).astype(jnp.int32)
        large = jnp.minimum(large, NUM_BUCKETS - 1)
        bucket = jnp.where(n < max_exact, n, large)
        val = jnp.zeros((TK, TQ), F32)
        for bkt in range(NUM_BUCKETS):
            val = jnp.where(bucket == bkt, rb_ref[bkt, mp], val)
        val = (val - rb_ref[NUM_BUCKETS - 1, mp]) * LOG2E
        if t == 0:
            val = jnp.where(krow <= qcol, val, NEG)
        o_ref[0, t] = val


def _bias_tiles(rel_bias):
    n_maps = 2 * DIFF_HEADS
    return pl.pallas_call(
        _bias_tile_kernel,
        grid=(n_maps,),
        in_specs=[pl.BlockSpec(memory_space=pltpu.SMEM)],
        out_specs=pl.BlockSpec((1, 2, TK, TQ), lambda mp: (mp, 0, 0, 0)),
        out_shape=jax.ShapeDtypeStruct((n_maps, 2, TK, TQ), F32),
        compiler_params=_params("arbitrary"),
        name="bias_tiles",
    )(rel_bias.reshape(NUM_BUCKETS, n_maps))


def _diff_attn_kernel(q1_ref, q2_ref, k1_ref, k2_ref, v_ref, bias_ref, lam_ref,
                      sub_ref, o_ref, s_ref, p_ref, vt_ref, *, lambda_init):
    lamf = lam_ref[...]
    lam_full = (jnp.exp(jnp.sum(lamf[0:1] * lamf[1:2], axis=-1, keepdims=True))
                - jnp.exp(jnp.sum(lamf[2:3] * lamf[3:4], axis=-1, keepdims=True))
                + lambda_init)
    _transpose_values(v_ref, slice(None), vt_ref)
    maps = ((q1_ref, k1_ref), (q2_ref, k2_ref))
    sums = {}

    def slot(i, mp):
        return 2 * (i % (DIFF_SLOTS // 2)) + mp

    def stage_scores(i):
        qs, keys = i * TQ, i * TQ + TK
        near = qs - TK
        for mp, (q_ref, k_ref) in enumerate(maps):
            sr = s_ref.at[slot(i, mp)]
            s = lax.dot_general(k_ref[0, :keys, :], q_ref[0, qs:qs + TQ, :], _NT,
                                preferred_element_type=F32)
            if i > 1:
                sr[:near, :] = s[:near]
            if i > 0:
                sr[near:qs, :] = s[near:qs] + bias_ref[mp, 1]
            sr[qs:keys, :] = s[qs:keys] + bias_ref[mp, 0]

    def stage_softmax(i):
        for mp in range(2):
            sl = slot(i, mp)
            sums[(i, mp)] = _softmax_columns(s_ref.at[sl], p_ref.at[sl], i + 1)

    def stage_output(i):
        qs, keys = i * TQ, i * TQ + TK
        outs = []
        for mp in range(2):
            pr = p_ref.at[slot(i, mp)]
            acc_t = jnp.dot(vt_ref[:, :keys], pr[:keys, :], preferred_element_type=F32)
            outs.append(acc_t * (1.0 / sums.pop((i, mp))))
        o = (outs[0] - lam_full * outs[1]).T
        ms = jnp.mean(o * o, axis=-1, keepdims=True)
        o = o * lax.rsqrt(ms + SUBLN_EPS) * sub_ref[...] * (1.0 - lambda_init)
        o_ref[0, qs:qs + TQ, :] = o.astype(BF16)

    _staged(range(NQ), stage_scores, stage_softmax, stage_output)


def _diff_attention(proj, bias_tiles, lam, subln, lambda_init):
    qk = lambda off, j: pl.BlockSpec((1, SEQ, HEAD_DIM),
                                     lambda h, b: (b, 0, off + 2 * h + j))
    n_maps = 2 * DIFF_HEADS
    width = 2 * HEAD_DIM
    return pl.pallas_call(
        functools.partial(_diff_attn_kernel, lambda_init=lambda_init),
        grid=(DIFF_HEADS, BATCH),
        in_specs=[qk(0, 0), qk(0, 1), qk(n_maps, 0), qk(n_maps, 1),
                  pl.BlockSpec((1, SEQ, width), lambda h, b: (b, 0, 2 * DIFF_HEADS + h)),
                  pl.BlockSpec((2, 2, TK, TQ), lambda h, b: (h, 0, 0, 0)),
                  pl.BlockSpec((4, HEAD_DIM), lambda h, b: (0, 0)),
                  pl.BlockSpec((1, width), lambda h, b: (0, 0))],
        out_specs=pl.BlockSpec((1, SEQ, width), lambda h, b: (b, 0, h)),
        out_shape=jax.ShapeDtypeStruct((BATCH, SEQ, D_MODEL), BF16),
        scratch_shapes=[pltpu.VMEM((DIFF_SLOTS, SEQ, TQ), F32),
                        pltpu.VMEM((DIFF_SLOTS, SEQ, TQ), BF16),
                        pltpu.VMEM((width, SEQ), BF16)],
        compiler_params=_params("arbitrary", "arbitrary"),
        name="diff_attn",
    )(proj, proj, proj, proj, proj, bias_tiles, lam, subln.reshape(1, width))


def kernel(x, attn_norm, ffn_norm, final_norm, fox_w_in, fox_b_f, fox_qk_norm, fox_w_out,
           diff_w_in, diff_lambda, diff_subln, diff_w_out, rel_bias,
           ffn_w_in, ffn_conv_w, ffn_conv_b, ffn_w_out):
    h = x.reshape(TOKENS, D_MODEL)
    bias_tiles = _bias_tiles(rel_bias)
    fox_w_in_t = jnp.swapaxes(fox_w_in, 1, 2)
    for i in range(DEPTH):
        j = i // 2
        if i % 2 == 0:
            a, cols = _fox_norm_decay(h, attn_norm[i], fox_w_in_t, j, fox_b_f[j])
            proj = _fox_proj(a, fox_w_in_t, j, fox_qk_norm[j])
            o = _fox_attention(proj.reshape(BATCH, SEQ, 4 * D_MODEL),
                               cols.reshape(BATCH, SEQ, D_MODEL))
            h, a = _attn_out(o.reshape(TOKENS, D_MODEL), fox_w_out, j, h, ffn_norm[i])
        else:
            lambda_init = 0.8 - 0.6 * math.exp(-0.3 * i)
            proj = _diff_proj(hb, ssq, attn_norm[i], diff_w_in, j)
            o = _diff_attention(proj.reshape(BATCH, SEQ, 3 * D_MODEL), bias_tiles,
                                diff_lambda[j], diff_subln[j], lambda_init)
            h, a = _attn_out(o.reshape(TOKENS, D_MODEL), diff_w_out, j, h, ffn_norm[i])
        y = _ffn_in(a, ffn_w_in, i, ffn_conv_w[i], ffn_conv_b[i])
        if i % 2 == 0:
            h, hb, ssq = _residual_matmul(y, ffn_w_out, i, h, "ffn_out", emit_stats=True)
        else:
            h = _residual_matmul(y, ffn_w_out, i, h, "ffn_out")
    return _rmsnorm(h, final_norm, F32).reshape(BATCH, SEQ, D_MODEL)
```
